```python
import math
import jax, jax.numpy as jnp
from jax import lax
import numpy as np

D_MODEL = 1024
BATCH = 4
SEQ = 4096
DEPTH = 2
DEC_BATCH = 32
DEC_SEQ = 8
PAST_LEN = 8192
PAGE_SIZE = 128

HEAD_DIM = 64
SB_HEADS = 8
NSA_HEADS = 8
NSA_KV_HEADS = 2
NSA_GROUP = NSA_HEADS // NSA_KV_HEADS
SB_WIDTH = SB_HEADS * HEAD_DIM
NSA_WIDTH = NSA_HEADS * HEAD_DIM
KV_WIDTH = NSA_KV_HEADS * HEAD_DIM
ROPE_DIM = HEAD_DIM // 4
ROPE_THETA = 500000.0
CMP_BLOCK = 32
CMP_STRIDE = 16
CMP_HIDDEN = 2 * HEAD_DIM
SEL_BLOCK = 64
SEL_TOPN = 16
SEL_FORCE = 1e4
WINDOW = 512
Q_BLOCK = 128
PEER_HEADS = 8
PEER_NKEYS = 128
PEER_EXPERTS = PEER_NKEYS * PEER_NKEYS
PEER_DK = 256
PEER_TOPK = 16
PEER_CHUNK = 256
DN_ALPHA = (2 * DEPTH) ** 0.25
DN_BETA = (8 * DEPTH) ** -0.25
LN_EPS = 1e-5
NEG = -1e30
IN_SIZES = (SB_WIDTH, SB_WIDTH, SB_WIDTH, NSA_WIDTH, KV_WIDTH, KV_WIDTH, KV_WIDTH, KV_WIDTH,
            KV_WIDTH, KV_WIDTH, 3 * NSA_HEADS, D_MODEL, D_MODEL)
IN_COLS = sum(IN_SIZES)

kernel_name = 'hybrid_sb_nsa_peer_decoder_step'


def layer_norm(x, g, b):
    xf = x.astype(jnp.float32)
    mu = xf.mean(-1, keepdims=True)
    var = jnp.square(xf - mu).mean(-1, keepdims=True)
    return ((xf - mu) * lax.rsqrt(var + LN_EPS) * g + b).astype(x.dtype)


def split_cols(h):
    out, start = [], 0
    for n in IN_SIZES:
        out.append(h[..., start:start + n])
        start += n
    return out


def partial_rope(x, pos):
    half = ROPE_DIM // 2
    inv_freq = ROPE_THETA ** (-jnp.arange(half, dtype=jnp.float32) * (2.0 / ROPE_DIM))
    ang = pos.astype(jnp.float32)[:, None] * inv_freq[None, :]
    cos = jnp.cos(ang)[None, :, None, :]
    sin = jnp.sin(ang)[None, :, None, :]
    xr = x[..., :ROPE_DIM].astype(jnp.float32)
    x1, x2 = xr[..., :half], xr[..., half:]
    rot = jnp.concatenate([x1 * cos - x2 * sin, x2 * cos + x1 * sin], axis=-1).astype(x.dtype)
    return jnp.concatenate([rot, x[..., ROPE_DIM:]], axis=-1)


def stick_breaking_attention(q, k, v, pos0):
    B, Lq, H, dh = q.shape
    Lk = k.shape[1]
    qb = math.gcd(Lq, Q_BLOCK)
    scale = dh ** -0.5
    kpos = jnp.arange(Lk)

    def block(i):
        start = i * qb
        qi = lax.dynamic_slice_in_dim(q, start, qb, axis=1)
        qpos = pos0 + start + jnp.arange(qb)
        z = jnp.einsum('bqhd,bkhd->bhqk', qi, k).astype(jnp.float32) * scale
        visible = kpos[None, :] < qpos[:, None]
        log_keep = jnp.where(visible, jax.nn.log_sigmoid(-z), 0.0)
        between = lax.cumsum(log_keep, axis=3, reverse=True) - log_keep
        w = jnp.where(visible, jnp.exp(jax.nn.log_sigmoid(z) + between), 0.0)
        return jnp.einsum('bhqk,bkhd->bqhd', w.astype(v.dtype), v)

    out = lax.map(block, jnp.arange(Lq // qb))
    return jnp.moveaxis(out, 0, 1).reshape(B, Lq, H, dh)


def compress_rows(rows, w1, w2, pe):
    L = rows.shape[1]
    n_cmp = (L - CMP_BLOCK) // CMP_STRIDE + 1
    idx = CMP_STRIDE * jnp.arange(n_cmp)[:, None] + jnp.arange(CMP_BLOCK)[None, :]
    blocks = rows[:, idx] + pe[None, None, :, None, :]
    hid = jax.nn.gelu(jnp.einsum('bnjgd,jdh->bngh', blocks, w1))
    return jnp.einsum('bngh,hd->bngd', hid, w2)


def nsa_attention(q, k_cmp, v_cmp, k_sel, v_sel, k_win, v_win, gates, pos0, w_cmp1, w_cmp2, cmp_pe):
    B, Lq, H, dh = q.shape
    L = k_cmp.shape[1]
    scale = dh ** -0.5
    qg = q.reshape(B, Lq, NSA_KV_HEADS, NSA_GROUP, dh)
    gg = gates.reshape(B, Lq, NSA_KV_HEADS, NSA_GROUP, 3)
    ck = compress_rows(k_cmp, w_cmp1[0], w_cmp2[0], cmp_pe[0])
    cv = compress_rows(v_cmp, w_cmp1[1], w_cmp2[1], cmp_pe[1])
    n_cmp = ck.shape[1]
    cmp_end = CMP_STRIDE * jnp.arange(n_cmp) + (CMP_BLOCK - 1)
    n_sel = -(-L // SEL_BLOCK)
    n_top = min(SEL_TOPN, n_sel)

    def to_blocks(t):
        t = jnp.pad(t, ((0, 0), (0, n_sel * SEL_BLOCK - L), (0, 0), (0, 0)))
        return t.reshape(B, n_sel, SEL_BLOCK, NSA_KV_HEADS, dh).transpose(0, 3, 1, 2, 4)

    ksb, vsb = to_blocks(k_sel), to_blocks(v_sel)
    c0 = (CMP_STRIDE * jnp.arange(n_cmp))[:, None]
    s0 = (SEL_BLOCK * jnp.arange(n_sel))[None, :]
    overlap = jnp.clip(jnp.minimum(c0 + CMP_BLOCK, s0 + SEL_BLOCK) - jnp.maximum(c0, s0), 0).astype(jnp.float32) * (1.0 / CMP_BLOCK)
    sel_ids = jnp.arange(n_sel)
    take_blocks = jax.vmap(jax.vmap(lambda blk, ix: blk[ix]))
    qb = math.gcd(Lq, Q_BLOCK)

    def block(i):
        start = i * qb
        qi = lax.dynamic_slice_in_dim(qg, start, qb, axis=1)
        qpos = pos0 + start + jnp.arange(qb)
        sc = jnp.einsum('bqkgd,bnkd->bkgqn', qi, ck).astype(jnp.float32) * scale
        cvis = cmp_end[None, :] <= qpos[:, None]
        pc = jax.nn.softmax(jnp.where(cvis, sc, NEG), axis=-1) * cvis
        o_c = jnp.einsum('bkgqn,bnkd->bqkgd', pc.astype(cv.dtype), cv)
        imp = jnp.einsum('bkgqn,ns->bkqs', pc, overlap)
        cur = (qpos // SEL_BLOCK)[:, None]
        forced = (sel_ids[None, :] == 0) | (sel_ids[None, :] == cur) | (sel_ids[None, :] == cur - 1)
        imp = jnp.where(sel_ids[None, :] <= cur, imp + SEL_FORCE * forced, NEG)
        top_s, top_i = lax.top_k(imp, n_top)
        tpos = top_i[..., None] * SEL_BLOCK + jnp.arange(SEL_BLOCK)
        svis = ((top_s > 0.5 * NEG)[..., None] & (tpos <= qpos[:, None, None])).reshape(B, NSA_KV_HEADS, qb, n_top * SEL_BLOCK)
        kg = take_blocks(ksb, top_i).reshape(B, NSA_KV_HEADS, qb, n_top * SEL_BLOCK, dh)
        vg = take_blocks(vsb, top_i).reshape(B, NSA_KV_HEADS, qb, n_top * SEL_BLOCK, dh)
        ss = jnp.einsum('bqkgd,bkqnd->bkgqn', qi, kg).astype(jnp.float32) * scale
        ps = jax.nn.softmax(jnp.where(svis[:, :, None], ss, NEG), axis=-1)
        o_s = jnp.einsum('bkgqn,bkqnd->bqkgd', ps.astype(vg.dtype), vg)
        kwi = lax.dynamic_slice_in_dim(k_win, start, qb + WINDOW, axis=1)
        vwi = lax.dynamic_slice_in_dim(v_win, start, qb + WINDOW, axis=1)
        wpos = pos0 - WINDOW + start + jnp.arange(qb + WINDOW)
        rel = qpos[:, None] - wpos[None, :]
        wvis = (rel >= 0) & (rel < WINDOW) & (wpos[None, :] >= 0)
        sw = jnp.einsum('bqkgd,bwkd->bkgqw', qi, kwi).astype(jnp.float32) * scale
        pw = jax.nn.softmax(jnp.where(wvis, sw, NEG), axis=-1)
        o_w = jnp.einsum('bkgqw,bwkd->bqkgd', pw.astype(vwi.dtype), vwi)
        g = lax.dynamic_slice_in_dim(gg, start, qb, axis=1).astype(o_c.dtype)
        return o_c * g[..., 0:1] + o_s * g[..., 1:2] + o_w * g[..., 2:3]

    out = lax.map(block, jnp.arange(Lq // qb))
    return jnp.moveaxis(out, 0, 1).reshape(B, Lq, H * dh)


def peer_ffn(x, wq, sub_keys, u, v):
    B, L, D = x.shape
    n = B * L
    chunk = min(PEER_CHUNK, n)
    n_chunks = -(-n // chunk)
    xt = jnp.pad(x.reshape(n, D), ((0, n_chunks * chunk - n), (0, 0))).reshape(n_chunks, chunk, D)

    def retrieve(xc):
        qh = (xc @ wq).reshape(chunk, PEER_HEADS, 2, PEER_DK // 2)
        s = jnp.einsum('chpe,hpne->chpn', qh, sub_keys).astype(jnp.float32)
        top_s, top_i = lax.top_k(s, PEER_TOPK)
        cand_s = (top_s[:, :, 0, :, None] + top_s[:, :, 1, None, :]).reshape(chunk, PEER_HEADS, -1)
        cand_i = (top_i[:, :, 0, :, None] * PEER_NKEYS + top_i[:, :, 1, None, :]).reshape(chunk, PEER_HEADS, -1)
        fin_s, fin_j = lax.top_k(cand_s, PEER_TOPK)
        expert = jnp.take_along_axis(cand_i, fin_j, axis=-1)
        gate = jax.nn.softmax(fin_s, axis=-1)
        act = jax.nn.gelu(jnp.einsum('cd,chkd->chk', xc, u[expert]).astype(jnp.float32))
        return jnp.einsum('chk,chkd->cd', (gate * act).astype(v.dtype), v[expert])

    y = lax.map(retrieve, xt).reshape(n_chunks * chunk, D)[:n]
    return y.reshape(B, L, D).astype(x.dtype)


def decoder_layer(x, pos0, past_sb, past_nsa, past_win, n_keep,
                  w_in, w_cmp1, w_cmp2, cmp_pe, w_br_a, w_br_b, w_out, ln1_g, ln1_b,
                  peer_wq, peer_keys, peer_u, peer_v, ln2_g, ln2_b):
    B, Lq, _ = x.shape
    pos = pos0 + jnp.arange(Lq)
    (sb_q, sb_k, sb_v, n_q, c_k, c_v, s_k, s_v, w_k, w_v, n_g, g_a, g_b) = split_cols(x @ w_in)
    hs = lambda t: t.reshape(B, Lq, -1, HEAD_DIM)
    new_sb = jnp.stack([hs(sb_k), hs(sb_v)], axis=2)
    sb_all = jnp.concatenate([past_sb, new_sb], axis=1)
    o_sb = stick_breaking_attention(hs(sb_q), sb_all[:, :, 0], sb_all[:, :, 1], pos0)
    new_nsa = jnp.stack([partial_rope(hs(c_k), pos), hs(c_v), partial_rope(hs(s_k), pos), hs(s_v)], axis=2)
    nsa_all = jnp.concatenate([past_nsa, new_nsa], axis=1)
    new_w = jnp.stack([partial_rope(hs(w_k), pos), hs(w_v)], axis=2)
    win_all = jnp.concatenate([past_win, new_w], axis=1)
    win_ext = jnp.pad(win_all, ((0, 0), (WINDOW - past_win.shape[1], 0), (0, 0), (0, 0), (0, 0)))
    gates = jax.nn.sigmoid(n_g.reshape(B, Lq, NSA_HEADS, 3))
    o_nsa = nsa_attention(partial_rope(hs(n_q), pos), nsa_all[:, :, 0], nsa_all[:, :, 1],
                          nsa_all[:, :, 2], nsa_all[:, :, 3], win_ext[:, :, 0], win_ext[:, :, 1],
                          gates, pos0, w_cmp1, w_cmp2, cmp_pe)
    merged = (jax.nn.sigmoid(g_a) * (o_sb.reshape(B, Lq, SB_WIDTH) @ w_br_a)
              + jax.nn.sigmoid(g_b) * (o_nsa @ w_br_b))
    x = layer_norm(DN_ALPHA * x + merged @ w_out, ln1_g, ln1_b)
    x = layer_norm(DN_ALPHA * x + peer_ffn(x, peer_wq, peer_keys, peer_u, peer_v), ln2_g, ln2_b)
    return x, new_sb, new_nsa, win_all[:, win_all.shape[1] - n_keep:]


def setup_inputs(seed: int = 0) -> dict:
    key = jax.random.key(seed)
    ks = jax.random.split(key, 24)
    n_pages = PAST_LEN // PAGE_SIZE
    n_used = DEC_BATCH * n_pages
    n_pool = n_used + max(1, n_used // 4)
    wbuf = min(WINDOW, PAST_LEN)
    nrm = lambda k, shape, scale: jax.random.normal(k, shape, jnp.float32) * scale
    page_table = jax.random.permutation(ks[5], n_pool)[:n_used].reshape(DEC_BATCH, n_pages).astype(jnp.int32)
    return {
        'x_prompt': nrm(ks[0], (BATCH, SEQ, D_MODEL), 1.0),
        'x_sample': nrm(ks[1], (DEC_BATCH, DEC_SEQ, D_MODEL), 1.0),
        'cache_sb': nrm(ks[2], (DEPTH, n_pool, PAGE_SIZE, 2, SB_HEADS, HEAD_DIM), 1.0),
        'cache_nsa': nrm(ks[3], (DEPTH, n_pool, PAGE_SIZE, 4, NSA_KV_HEADS, HEAD_DIM), 1.0),
        'state_win': nrm(ks[4], (DEPTH, DEC_BATCH, wbuf, 2, NSA_KV_HEADS, HEAD_DIM), 1.0),
        'page_table': page_table,
        'w_in': nrm(ks[6], (DEPTH, D_MODEL, IN_COLS), D_MODEL ** -0.5),
        'w_cmp1': nrm(ks[7], (DEPTH, 2, CMP_BLOCK, HEAD_DIM, CMP_HIDDEN), (CMP_BLOCK * HEAD_DIM) ** -0.5),
        'w_cmp2': nrm(ks[8], (DEPTH, 2, CMP_HIDDEN, HEAD_DIM), CMP_HIDDEN ** -0.5),
        'cmp_pe': nrm(ks[9], (DEPTH, 2, CMP_BLOCK, HEAD_DIM), 0.1),
        'w_br_a': nrm(ks[10], (DEPTH, SB_WIDTH, D_MODEL), SB_WIDTH ** -0.5),
        'w_br_b': nrm(ks[11], (DEPTH, NSA_WIDTH, D_MODEL), NSA_WIDTH ** -0.5),
        'w_out': nrm(ks[12], (DEPTH, D_MODEL, D_MODEL), DN_BETA * D_MODEL ** -0.5),
        'ln1_g': 1.0 + nrm(ks[13], (DEPTH, D_MODEL), 0.01),
        'ln1_b': nrm(ks[14], (DEPTH, D_MODEL), 0.01),
        'peer_wq': nrm(ks[15], (DEPTH, D_MODEL, PEER_HEADS * PEER_DK), D_MODEL ** -0.5),
        'peer_keys': nrm(ks[16], (DEPTH, PEER_HEADS, 2, PEER_NKEYS, PEER_DK // 2), (PEER_DK // 2) ** -0.5),
        'peer_u': nrm(ks[17], (DEPTH, PEER_EXPERTS, D_MODEL), D_MODEL ** -0.5),
        'peer_v': nrm(ks[18], (DEPTH, PEER_EXPERTS, D_MODEL), DN_BETA * PEER_HEADS ** -0.5),
        'ln2_g': 1.0 + nrm(ks[19], (DEPTH, D_MODEL), 0.01),
        'ln2_b': nrm(ks[20], (DEPTH, D_MODEL), 0.01),
    }


def reference(x_prompt, x_sample, cache_sb, cache_nsa, state_win, page_table,
              w_in, w_cmp1, w_cmp2, cmp_pe, w_br_a, w_br_b, w_out, ln1_g, ln1_b,
              peer_wq, peer_keys, peer_u, peer_v, ln2_g, ln2_b):
    bp, seq = x_prompt.shape[0], x_prompt.shape[1]
    bd = x_sample.shape[0]
    past_len = page_table.shape[1] * PAGE_SIZE
    dt = x_prompt.dtype
    empty_sb = jnp.zeros((bp, 0, 2, SB_HEADS, HEAD_DIM), dt)
    empty_nsa = jnp.zeros((bp, 0, 4, NSA_KV_HEADS, HEAD_DIM), dt)
    empty_win = jnp.zeros((bp, 0, 2, NSA_KV_HEADS, HEAD_DIM), dt)
    keep_prompt = min(WINDOW, seq)
    keep_sample = state_win.shape[2]
    xp, xs = x_prompt, x_sample
    sb_p, sb_s, nsa_p, nsa_s, win_p, win_s = [], [], [], [], [], []
    for l in range(DEPTH):
        lw = [a[l] for a in (w_in, w_cmp1, w_cmp2, cmp_pe, w_br_a, w_br_b, w_out, ln1_g, ln1_b,
                             peer_wq, peer_keys, peer_u, peer_v, ln2_g, ln2_b)]
        xp, r_sb, r_nsa, r_win = decoder_layer(xp, 0, empty_sb, empty_nsa, empty_win, keep_prompt, *lw)
        sb_p.append(r_sb); nsa_p.append(r_nsa); win_p.append(r_win)
        past_sb = cache_sb[l][page_table].reshape(bd, past_len, 2, SB_HEADS, HEAD_DIM)
        past_nsa = cache_nsa[l][page_table].reshape(bd, past_len, 4, NSA_KV_HEADS, HEAD_DIM)
        xs, r_sb, r_nsa, r_win = decoder_layer(xs, past_len, past_sb, past_nsa, state_win[l], keep_sample, *lw)
        sb_s.append(r_sb); nsa_s.append(r_nsa); win_s.append(r_win)
    return (xp, xs, jnp.stack(sb_p), jnp.stack(sb_s), jnp.stack(nsa_p), jnp.stack(nsa_s), jnp.stack(win_p), jnp.stack(win_s))
```

```python
import functools
import math

import jax
import jax.numpy as jnp
from jax import lax
from jax.experimental import pallas as pl
from jax.experimental.pallas import tpu as pltpu

F32 = jnp.float32
BF16 = jnp.bfloat16

D_MODEL = 1024
PAGE_SIZE = 128
HEAD_DIM = 64
SB_HEADS = 8
NSA_HEADS = 8
NSA_KV_HEADS = 2
NSA_GROUP = NSA_HEADS // NSA_KV_HEADS
SB_WIDTH = SB_HEADS * HEAD_DIM
NSA_WIDTH = NSA_HEADS * HEAD_DIM
KV_WIDTH = NSA_KV_HEADS * HEAD_DIM
ROPE_DIM = HEAD_DIM // 4
ROPE_THETA = 500000.0
CMP_BLOCK = 32
CMP_STRIDE = 16
CMP_HIDDEN = 2 * HEAD_DIM
SEL_BLOCK = 64
SEL_TOPN = 16
SEL_FORCE = 1e4
WINDOW = 512
PEER_HEADS = 8
PEER_NKEYS = 128
PEER_DK = 256
PEER_TOPK = 16
DEPTH = 2
DN_ALPHA = (2 * DEPTH) ** 0.25
LN_EPS = 1e-5
NEG = -1e30
BELOW_NEG = -3e38
ATT_SCALE = HEAD_DIM ** -0.5

LANES = 128
SUBLANES = 8
VMEM_LIMIT = 56 * 1024 * 1024

N_GATE_COLS = 3 * NSA_HEADS
PROJ_COLS = 2816 + 2 * D_MODEL + LANES


def _cparams(sem):
    return pltpu.CompilerParams(dimension_semantics=sem, vmem_limit_bytes=VMEM_LIMIT)


def _nt_dot(a, b):
    return lax.dot_general(a, b, (((1,), (1,)), ((), ())), preferred_element_type=F32)


def _dot(a, b):
    return jnp.dot(a, b, preferred_element_type=F32)


def _gelu(x):
    c = math.sqrt(2.0 / math.pi)
    return 0.5 * x * (1.0 + jnp.tanh(c * (x + 0.044715 * (x * x * x))))


def _sigmoid(x):
    return 1.0 / (1.0 + jnp.exp(-x))


def _softplus(z):
    return jnp.maximum(z, 0.0) + jnp.log1p(jnp.exp(-jnp.abs(z)))


def _layer_norm(y, g, b):
    mu = jnp.mean(y, axis=-1, keepdims=True)
    d = y - mu
    var = jnp.mean(d * d, axis=-1, keepdims=True)
    return d * lax.rsqrt(var + LN_EPS) * g + b


def _proj_kernel(x_ref, w_ref, cos_ref, sa_ref, sb_ref,
                 sbq_ref, sbkv_ref, nq_ref, nkv_ref, win_ref, gab_ref, gates_ref):
    x = x_ref[...].astype(BF16)
    cos, sa, sb = cos_ref[...], sa_ref[...], sb_ref[...]

    def mm(c0, c1):
        return _dot(x, w_ref[:, c0:c1])

    def rope(t):
        half = ROPE_DIM // 2
        return t * cos + pltpu.roll(t, half, 1) * sa + pltpu.roll(t, LANES - half, 1) * sb

    sbq_ref[...] = mm(0, 512)
    sbkv_ref[...] = mm(512, 1536)
    hq = mm(1536, 2048)
    for i in range(4):
        nq_ref[:, i * LANES:(i + 1) * LANES] = rope(hq[:, i * LANES:(i + 1) * LANES])
    hk = mm(2048, 2560)
    nkv_ref[:, 0:128] = rope(hk[:, 0:128])
    nkv_ref[:, 128:256] = hk[:, 128:256]
    nkv_ref[:, 256:384] = rope(hk[:, 256:384])
    nkv_ref[:, 384:512] = hk[:, 384:512]
    hw = mm(2560, 2816)
    win_ref[:, 0:128] = rope(hw[:, 0:128])
    win_ref[:, 128:256] = hw[:, 128:256]
    gab_ref[:, 0:1024] = _sigmoid(mm(2816, 3840))
    gab_ref[:, 1024:2048] = _sigmoid(mm(3840, 4864))
    gates_ref[...] = _sigmoid(mm(4864, 4992))


def _proj(x, w, cos, sa, sb, tm=256):
    nt = x.shape[0]
    row = lambda n: pl.BlockSpec((tm, n), lambda i: (i, 0))
    outs = [(512, F32), (1024, F32), (512, F32), (512, F32), (256, F32), (2048, F32), (128, F32)]
    return pl.pallas_call(
        _proj_kernel,
        grid=(nt // tm,),
        in_specs=[row(D_MODEL), pl.BlockSpec((D_MODEL, PROJ_COLS), lambda i: (0, 0)),
                  row(LANES), row(LANES), row(LANES)],
        out_specs=[row(n) for n, _ in outs],
        out_shape=[jax.ShapeDtypeStruct((nt, n), dt) for n, dt in outs],
        compiler_params=_cparams(("parallel",)),
        name="in_proj",
    )(x, w, cos, sa, sb)


def _sb_block(q, k, v, tri, vis, carry):
    z = _nt_dot(q, k)
    sp = _softplus(z)
    lk = jnp.where(vis, -sp, 0.0)
    hi = lk.astype(BF16)
    lo = (lk - hi.astype(F32)).astype(BF16)
    between = _dot(hi, tri) + _dot(lo, tri) + carry
    w = jnp.where(vis, jnp.exp(z - sp + between), 0.0)
    return _dot(w.astype(BF16), v), carry + jnp.sum(lk, axis=1, keepdims=True)


def _sb_prompt_kernel(q_ref, k_ref, v_ref, tri_ref, o_ref, *, tq):
    i = pl.program_id(1)
    q = (q_ref[0] * ATT_SCALE).astype(BF16)
    tri = tri_ref[...]
    qpos = i * tq + lax.broadcasted_iota(jnp.int32, (tq, 1), 0)
    kio = lax.broadcasted_iota(jnp.int32, (1, tq), 1)

    def body(jj, c):
        acc, car = c
        j = i - jj
        start = pl.multiple_of(j * tq, tq)
        k = k_ref[0, pl.ds(start, tq), :].astype(BF16)
        v = v_ref[0, pl.ds(start, tq), :].astype(BF16)
        vis = (start + kio) < qpos
        contrib, car = _sb_block(q, k, v, tri, vis, car)
        return acc + contrib, car

    acc, _ = lax.fori_loop(0, i + 1, body,
                           (jnp.zeros((tq, HEAD_DIM), F32), jnp.zeros((tq, 1), F32)))
    o_ref[0] = acc


def _strict_lower(n):
    r = lax.broadcasted_iota(jnp.int32, (n, n), 0)
    c = lax.broadcasted_iota(jnp.int32, (n, n), 1)
    return (r > c).astype(BF16)


def _sb_prompt(q, k, v, tq=256):
    bh, L, _ = q.shape
    tq = min(tq, L)
    full = pl.BlockSpec((1, L, HEAD_DIM), lambda b, i: (b, 0, 0))
    return pl.pallas_call(
        functools.partial(_sb_prompt_kernel, tq=tq),
        grid=(bh, L // tq),
        in_specs=[pl.BlockSpec((1, tq, HEAD_DIM), lambda b, i: (b, i, 0)), full, full,
                  pl.BlockSpec((tq, tq), lambda b, i: (0, 0))],
        out_specs=pl.BlockSpec((1, tq, HEAD_DIM), lambda b, i: (b, i, 0)),
        out_shape=jax.ShapeDtypeStruct((bh, L, HEAD_DIM), F32),
        compiler_params=_cparams(("parallel", "arbitrary")),
        name="sb_prompt",
    )(q, k, v, _strict_lower(tq))


def _sb_decode_kernel(pt_ref, q_ref, kn_ref, vn_ref, kp_ref, vp_ref, tri_ref, o_ref,
                      acc_ref, car_ref, *, n_pages, lq):
    j = pl.program_id(1)
    rows = SB_HEADS * lq
    past = n_pages * PAGE_SIZE

    @pl.when(j == 0)
    def _():
        acc_ref[...] = jnp.zeros_like(acc_ref)
        car_ref[...] = jnp.zeros_like(car_ref)

    q = (q_ref[0] * ATT_SCALE).astype(BF16)
    qpos = past + lax.broadcasted_iota(jnp.int32, (rows, 1), 0) % lq
    kio = lax.broadcasted_iota(jnp.int32, (1, PAGE_SIZE), 1)

    def step(k, v, base):
        vis = (base + kio) < qpos
        contrib, car = _sb_block(q, k.astype(BF16), v.astype(BF16), tri_ref[...], vis,
                                 car_ref[:, 0:1])
        acc_ref[...] += contrib
        car_ref[...] = jnp.broadcast_to(car, car_ref.shape)

    @pl.when(j == 0)
    def _():
        step(kn_ref[0], vn_ref[0], past)

    @pl.when(j > 0)
    def _():
        step(kp_ref[0], vp_ref[0], (n_pages - j) * PAGE_SIZE)

    @pl.when(j == n_pages)
    def _():
        o_ref[0] = acc_ref[...]


def _sb_decode(page_table, qm, k_new, v_new, cache, lq):
    bd, n_pages = page_table.shape
    rows = SB_HEADS * lq
    page = lambda lane_blk: pl.BlockSpec(
        (1, PAGE_SIZE, SB_WIDTH),
        lambda b, j, pt: (pt[b, jnp.minimum(n_pages - j, n_pages - 1)], 0, lane_blk))
    per_b = lambda r, c: pl.BlockSpec((1, r, c), lambda b, j, pt: (b, 0, 0))
    grid_spec = pltpu.PrefetchScalarGridSpec(
        num_scalar_prefetch=1,
        grid=(bd, n_pages + 1),
        in_specs=[per_b(rows, SB_WIDTH), per_b(PAGE_SIZE, SB_WIDTH), per_b(PAGE_SIZE, SB_WIDTH),
                  page(0), page(1),
                  pl.BlockSpec((PAGE_SIZE, PAGE_SIZE), lambda b, j, pt: (0, 0))],
        out_specs=per_b(rows, SB_WIDTH),
        scratch_shapes=[pltpu.VMEM((rows, SB_WIDTH), F32), pltpu.VMEM((rows, LANES), F32)],
    )
    return pl.pallas_call(
        functools.partial(_sb_decode_kernel, n_pages=n_pages, lq=lq),
        grid_spec=grid_spec,
        out_shape=jax.ShapeDtypeStruct((bd, rows, SB_WIDTH), F32),
        compiler_params=_cparams(("parallel", "arbitrary")),
        name="sb_decode",
    )(page_table, qm, k_new, v_new, cache, cache, _strict_lower(PAGE_SIZE))


def _cmp_partial_kernel(rk_ref, rv_ref, wa_ref, wb_ref, pea_ref, peb_ref, o_ref, xk_ref, xv_ref,
                        *, pg):
    nchunk = PAGE_SIZE // CMP_STRIDE

    def gather(p, c):
        dst = pl.ds(pl.multiple_of(p * nchunk, nchunk), nchunk)
        for j in range(CMP_STRIDE):
            src = pl.ds(j, nchunk, stride=CMP_STRIDE)
            xk_ref[dst, j * LANES:(j + 1) * LANES] = rk_ref[p, src, :]
            xv_ref[dst, j * LANES:(j + 1) * LANES] = rv_ref[p, src, :]
        return c

    lax.fori_loop(0, pg, gather, 0)
    for t, x_ref in enumerate((xk_ref, xv_ref)):
        x = x_ref[...]
        a = _dot((x + pea_ref[t]).astype(BF16), wa_ref[t])
        b = _dot((x + peb_ref[t]).astype(BF16), wb_ref[t])
        o_ref[:, t * 512:t * 512 + 256] = a
        o_ref[:, t * 512 + 256:(t + 1) * 512] = b


def _cmp_partial(rows, wa, wb, pea, peb, pg):
    n_pages = rows.shape[0]
    nchunk = PAGE_SIZE // CMP_STRIDE
    kdim = CMP_STRIDE * LANES
    const3 = lambda s: pl.BlockSpec(s, lambda i: (0, 0, 0))
    return pl.pallas_call(
        functools.partial(_cmp_partial_kernel, pg=pg),
        grid=(n_pages // pg,),
        in_specs=[pl.BlockSpec((pg, PAGE_SIZE, LANES), lambda i: (i, 0, 0)),
                  pl.BlockSpec((pg, PAGE_SIZE, LANES), lambda i: (i, 0, 1)),
                  const3((2, kdim, 2 * LANES)), const3((2, kdim, 2 * LANES)),
                  const3((2, 1, kdim)), const3((2, 1, kdim))],
        out_specs=pl.BlockSpec((pg * nchunk, 1024), lambda i: (i, 0)),
        out_shape=jax.ShapeDtypeStruct((n_pages * nchunk, 1024), F32),
        scratch_shapes=[pltpu.VMEM((pg * nchunk, kdim), F32), pltpu.VMEM((pg * nchunk, kdim), F32)],
        compiler_params=_cparams(("parallel",)),
        name="cmp_partial",
    )(rows, rows, wa, wb, pea, peb)


def _cmp_finish_kernel(pt_ref, ab_ref, w2_ref, ck_ref, cv_ref, buf_ref, sem_ref, *, pp):
    b = pl.program_id(0)
    nchunk = PAGE_SIZE // CMP_STRIDE

    def copy(j):
        return pltpu.make_async_copy(ab_ref.at[pt_ref[b, j]], buf_ref.at[j], sem_ref.at[j])

    for j in range(pp):
        copy(j).start()
    for j in range(pp):
        copy(j).wait()

    n = pp * nchunk
    x = buf_ref[...].reshape(n, 1024)
    last = lax.broadcasted_iota(jnp.int32, (n, 1), 0) == n - 1
    for t, o_ref in enumerate((ck_ref, cv_ref)):
        a = x[:, t * 512:t * 512 + 256]
        bnext = pltpu.roll(x[:, t * 512 + 256:(t + 1) * 512], n - 1, 0)
        hid = _gelu(a + jnp.where(last, 0.0, bnext))
        o_ref[0] = _dot(hid.astype(BF16), w2_ref[t])


def _cmp_finish(page_table, ab, w2big):
    nb, pp = page_table.shape
    nchunk = PAGE_SIZE // CMP_STRIDE
    n = pp * nchunk
    grid_spec = pltpu.PrefetchScalarGridSpec(
        num_scalar_prefetch=1,
        grid=(nb,),
        in_specs=[pl.BlockSpec(memory_space=pl.ANY),
                  pl.BlockSpec((2, 2 * LANES, LANES), lambda b, pt: (0, 0, 0))],
        out_specs=[pl.BlockSpec((1, n, LANES), lambda b, pt: (b, 0, 0))] * 2,
        scratch_shapes=[pltpu.VMEM((pp, nchunk, 1024), F32), pltpu.SemaphoreType.DMA((pp,))],
    )
    return pl.pallas_call(
        functools.partial(_cmp_finish_kernel, pp=pp),
        grid_spec=grid_spec,
        out_shape=[jax.ShapeDtypeStruct((nb, n, LANES), F32)] * 2,
        compiler_params=_cparams(("arbitrary",)),
        name="cmp_finish",
    )(page_table, ab, w2big)


def _topk_mask(score, n_top):
    r, s = score.shape
    lane = lax.broadcasted_iota(jnp.int32, (r, s), 1).astype(F32)
    sel = jnp.zeros((r, s), F32)
    work = score
    for _ in range(n_top):
        m = jnp.max(work, axis=-1, keepdims=True)
        idx = jnp.min(jnp.where(work == m, lane, float(s)), axis=-1, keepdims=True)
        hit = lane == idx
        sel = jnp.where(hit & (m > 0.5 * NEG), 1.0, sel)
        work = jnp.where(hit, BELOW_NEG, work)
    return sel


def _cmp_branch(q, ck, cv, ov, qpos):
    sc = _nt_dot(q, ck.astype(BF16))
    nio = lax.broadcasted_iota(jnp.int32, (1, sc.shape[1]), 1)
    cvis = (CMP_STRIDE * nio + (CMP_BLOCK - 1)) <= qpos
    scm = jnp.where(cvis, sc, NEG)
    e = jnp.exp(scm - jnp.max(scm, axis=-1, keepdims=True))
    pc = jnp.where(cvis, e / jnp.sum(e, axis=-1, keepdims=True), 0.0).astype(BF16)
    return _dot(pc, cv.astype(BF16)), _dot(pc, ov)


def _select_blocks(imp, qpos, n_top):
    sio = lax.broadcasted_iota(jnp.int32, (1, imp.shape[1]), 1)
    cur = qpos // SEL_BLOCK
    forced = (sio == 0) | (sio == cur) | (sio == cur - 1)
    imp = jnp.where(sio <= cur, imp + jnp.where(forced, SEL_FORCE, 0.0), NEG)
    return _topk_mask(imp, n_top)


def _softmax_step(state, s, mask, v):
    m, l, acc = state
    sm = jnp.where(mask, s, NEG)
    m_new = jnp.maximum(m, jnp.max(sm, axis=-1, keepdims=True))
    alpha = jnp.exp(m - m_new)
    p = jnp.where(mask, jnp.exp(sm - m_new), 0.0)
    return (m_new, alpha * l + jnp.sum(p, axis=-1, keepdims=True),
            alpha * acc + _dot(p.astype(BF16), v))


def _softmax_init(r):
    return (jnp.full((r, 1), NEG, F32), jnp.zeros((r, 1), F32), jnp.zeros((r, LANES), F32))


def _nsa_prompt_kernel(q_ref, ck_ref, cv_ref, ks_ref, vs_ref, kw_ref, vw_ref, g_ref, ov_ref,
                       o_ref, *, tq, n_top):
    i = pl.program_id(2)
    r = NSA_GROUP * tq
    q = (q_ref[0, 0].reshape(r, LANES) * ATT_SCALE).astype(BF16)
    qpos = i * tq + lax.broadcasted_iota(jnp.int32, (r, 1), 0) % tq
    qpos1 = i * tq + lax.broadcasted_iota(jnp.int32, (tq, 1), 0)

    o_c, impg = _cmp_branch(q, ck_ref[0], cv_ref[0], ov_ref[...], qpos)
    imp = impg[0:tq]
    for g in range(1, NSA_GROUP):
        imp = imp + impg[g * tq:(g + 1) * tq]
    sel = _select_blocks(imp, qpos1, n_top).astype(BF16)
    ns = sel.shape[1]

    tk = 2 * tq
    per = tk // SEL_BLOCK
    sio = lax.broadcasted_iota(jnp.int32, (ns, tk), 0)
    kio_s = lax.broadcasted_iota(jnp.int32, (ns, tk), 1) // SEL_BLOCK
    kio = lax.broadcasted_iota(jnp.int32, (1, tk), 1)

    def sel_body(j, state):
        start = pl.multiple_of(j * tk, tk)
        k = ks_ref[0, pl.ds(start, tk), :].astype(BF16)
        v = vs_ref[0, pl.ds(start, tk), :].astype(BF16)
        expand = (sio == kio_s + j * per).astype(BF16)
        picked = _dot(sel, expand)
        picked = jnp.concatenate([picked] * NSA_GROUP, axis=0)
        mask = (picked > 0.5) & ((start + kio) <= qpos)
        return _softmax_step(state, _nt_dot(q, k), mask, v)

    n_sel_steps = (i * tq + tq - 1) // tk + 1
    m_s, l_s, acc_s = lax.fori_loop(0, n_sel_steps, sel_body, _softmax_init(r))

    wio = lax.broadcasted_iota(jnp.int32, (1, tq), 1)

    def win_body(j, state):
        start = pl.multiple_of(j * tq, tq)
        k = kw_ref[0, pl.ds(start, tq), :].astype(BF16)
        v = vw_ref[0, pl.ds(start, tq), :].astype(BF16)
        rel = qpos - (start + wio)
        mask = (rel >= 0) & (rel < WINDOW)
        return _softmax_step(state, _nt_dot(q, k), mask, v)

    m_w, l_w, acc_w = lax.fori_loop(jnp.maximum(i - WINDOW // tq, 0), i + 1, win_body,
                                    _softmax_init(r))

    g = g_ref[0, 0].reshape(r, 3)
    o = o_c * g[:, 0:1] + (acc_s / l_s) * g[:, 1:2] + (acc_w / l_w) * g[:, 2:3]
    o_ref[0, 0] = o.reshape(NSA_GROUP, tq, LANES)


def _nsa_prompt(qm, ck, cv, nkv, win, gates, ov, n_top, tq=128):
    B, _, _, L, _ = qm.shape
    nc = ck.shape[1]
    seq = lambda lane_blk: pl.BlockSpec((1, L, LANES), lambda b, h, i: (b, 0, lane_blk))
    qspec = lambda last: pl.BlockSpec((1, 1, NSA_GROUP, tq, last), lambda b, h, i: (b, h, 0, i, 0))
    cspec = pl.BlockSpec((1, nc, LANES), lambda b, h, i: (b, 0, 0))
    return pl.pallas_call(
        functools.partial(_nsa_prompt_kernel, tq=tq, n_top=n_top),
        grid=(B, NSA_KV_HEADS, L // tq),
        in_specs=[qspec(LANES), cspec, cspec, seq(2), seq(3), seq(0), seq(1), qspec(3),
                  pl.BlockSpec(ov.shape, lambda b, h, i: (0, 0))],
        out_specs=qspec(LANES),
        out_shape=jax.ShapeDtypeStruct(qm.shape, F32),
        compiler_params=_cparams(("parallel", "parallel", "arbitrary")),
        name="nsa_prompt",
    )(qm, ck, cv, nkv, nkv, win, win, gates, ov)


def _nsa_decode_kernel(pt_ref, q_ref, ck_ref, cv_ref, kn_ref, vn_ref, kp_ref, vp_ref,
                       kw_ref, vw_ref, g_ref, ov_ref, o_ref,
                       sel_ref, oc_ref, m_ref, l_ref, acc_ref, *, n_pages, lq, n_top):
    j = pl.program_id(1)
    r = NSA_HEADS * lq
    rk = NSA_KV_HEADS * lq
    past = n_pages * PAGE_SIZE
    q = (q_ref[0] * ATT_SCALE).astype(BF16)
    qpos = past + lax.broadcasted_iota(jnp.int32, (r, 1), 0) % lq
    kio = lax.broadcasted_iota(jnp.int32, (1, PAGE_SIZE), 1)

    @pl.when(j == 0)
    def _():
        o_c, impg = _cmp_branch(q, ck_ref[0], cv_ref[0], ov_ref[...], qpos)
        oc_ref[...] = o_c
        ns = impg.shape[1]
        impg = impg.reshape(NSA_KV_HEADS, NSA_GROUP, lq, ns)
        imp = impg[:, 0]
        for g in range(1, NSA_GROUP):
            imp = imp + impg[:, g]
        qpos1 = past + lax.broadcasted_iota(jnp.int32, (rk, 1), 0) % lq
        sel_ref[...] = _select_blocks(imp.reshape(rk, ns), qpos1, n_top)
        m0, l0, a0 = _softmax_init(r)
        m_ref[...] = jnp.broadcast_to(m0, m_ref.shape)
        l_ref[...] = jnp.broadcast_to(l0, l_ref.shape)
        acc_ref[...] = a0

    def step(k, v, mask):
        state = (m_ref[:, 0:1], l_ref[:, 0:1], acc_ref[...])
        m, l, acc = _softmax_step(state, _nt_dot(q, k.astype(BF16)), mask, v.astype(BF16))
        m_ref[...] = jnp.broadcast_to(m, m_ref.shape)
        l_ref[...] = jnp.broadcast_to(l, l_ref.shape)
        acc_ref[...] = acc

    def picked_mask(blk):
        sel = sel_ref[...]
        sio = lax.broadcasted_iota(jnp.int32, sel.shape, 1)
        per = PAGE_SIZE // SEL_BLOCK
        lo = jnp.sum(jnp.where(sio == per * blk, sel, 0.0), axis=-1, keepdims=True)
        hi = jnp.sum(jnp.where(sio == per * blk + 1, sel, 0.0), axis=-1, keepdims=True)
        pk = jnp.where(kio < SEL_BLOCK, lo, hi)
        pk = jnp.broadcast_to(pk.reshape(NSA_KV_HEADS, 1, lq, PAGE_SIZE),
                              (NSA_KV_HEADS, NSA_GROUP, lq, PAGE_SIZE)).reshape(r, PAGE_SIZE)
        return pk > 0.5

    @pl.when(j < n_pages)
    def _():
        step(kp_ref[0], vp_ref[0], picked_mask(j))

    @pl.when(j == n_pages)
    def _():
        step(kn_ref[0], vn_ref[0], picked_mask(j) & ((past + kio) <= qpos))
        sel_out = acc_ref[...] / l_ref[:, 0:1]
        state = _softmax_init(r)
        nwin = kw_ref.shape[1] // PAGE_SIZE
        for w in range(nwin):
            k = kw_ref[0, w * PAGE_SIZE:(w + 1) * PAGE_SIZE, :].astype(BF16)
            v = vw_ref[0, w * PAGE_SIZE:(w + 1) * PAGE_SIZE, :].astype(BF16)
            rel = qpos - (past - WINDOW + w * PAGE_SIZE + kio)
            state = _softmax_step(state, _nt_dot(q, k), (rel >= 0) & (rel < WINDOW), v)
        _, l_w, acc_w = state
        g = g_ref[0]
        o_ref[0] = oc_ref[...] * g[:, 0:1] + sel_out * g[:, 1:2] + (acc_w / l_w) * g[:, 2:3]


def _nsa_decode(page_table, qm, ck, cv, k_new, v_new, cache, kwin, vwin, gates, ov, lq, n_top):
    bd, n_pages = page_table.shape
    r = NSA_HEADS * lq
    nc, ns = ov.shape
    per_b = lambda a: pl.BlockSpec((1,) + a.shape[1:], lambda b, j, pt: (b, 0, 0))
    page = lambda lane_blk: pl.BlockSpec(
        (1, PAGE_SIZE, LANES), lambda b, j, pt: (pt[b, jnp.minimum(j, n_pages - 1)], 0, lane_blk))
    grid_spec = pltpu.PrefetchScalarGridSpec(
        num_scalar_prefetch=1,
        grid=(bd, n_pages + 1),
        in_specs=[per_b(qm), per_b(ck), per_b(cv), per_b(k_new), per_b(v_new), page(2), page(3),
                  per_b(kwin), per_b(vwin), per_b(gates),
                  pl.BlockSpec((nc, ns), lambda b, j, pt: (0, 0))],
        out_specs=pl.BlockSpec((1, r, LANES), lambda b, j, pt: (b, 0, 0)),
        scratch_shapes=[pltpu.VMEM((NSA_KV_HEADS * lq, ns), F32), pltpu.VMEM((r, LANES), F32),
                        pltpu.VMEM((r, LANES), F32), pltpu.VMEM((r, LANES), F32),
                        pltpu.VMEM((r, LANES), F32)],
    )
    return pl.pallas_call(
        functools.partial(_nsa_decode_kernel, n_pages=n_pages, lq=lq, n_top=n_top),
        grid_spec=grid_spec,
        out_shape=jax.ShapeDtypeStruct((bd, r, LANES), F32),
        compiler_params=_cparams(("parallel", "arbitrary")),
        name="nsa_decode",
    )(page_table, qm, ck, cv, k_new, v_new, cache, cache, kwin, vwin, gates, ov)


def _merge_kernel(x_ref, oa_ref, ob_ref, gab_ref, wa_ref, wb_ref, wo_ref, g_ref, b_ref, y_ref):
    a = _dot(oa_ref[...].astype(BF16), wa_ref[...])
    b = _dot(ob_ref[...].astype(BF16), wb_ref[...])
    merged = gab_ref[:, 0:D_MODEL] * a + gab_ref[:, D_MODEL:2 * D_MODEL] * b
    y = DN_ALPHA * x_ref[...] + _dot(merged.astype(BF16), wo_ref[...])
    y_ref[...] = _layer_norm(y, g_ref[...], b_ref[...])


def _merge(x, oa, ob, gab, wa, wb, wo, g, b, tm=256):
    nt = x.shape[0]
    row = lambda n: pl.BlockSpec((tm, n), lambda i: (i, 0))
    const = lambda a: pl.BlockSpec(a.shape, lambda i: (0, 0))
    return pl.pallas_call(
        _merge_kernel,
        grid=(nt // tm,),
        in_specs=[row(D_MODEL), row(SB_WIDTH), row(NSA_WIDTH), row(2 * D_MODEL),
                  const(wa), const(wb), const(wo), const(g), const(b)],
        out_specs=row(D_MODEL),
        out_shape=jax.ShapeDtypeStruct((nt, D_MODEL), F32),
        compiler_params=_cparams(("parallel",)),
        name="merge_ln1",
    )(x, oa, ob, gab, wa, wb, wo, g, b)


def _top_values(x, k):
    n = x.shape[0]
    rio = lax.broadcasted_iota(jnp.int32, x.shape, 0).astype(F32)
    vals = []
    for _ in range(k):
        m = jnp.max(x, axis=0, keepdims=True)
        idx = jnp.min(jnp.where(x == m, rio, float(n)), axis=0, keepdims=True)
        x = jnp.where(rio == idx, BELOW_NEG, x)
        vals.append(m)
    return vals


def _pair_products(a, b):
    rows = [a[i] * b[j] for i in range(len(a)) for j in range(len(b))
            if (i + 1) * (j + 1) <= PEER_TOPK]
    pad = (-len(rows)) % SUBLANES
    rows += [jnp.full_like(rows[0], BELOW_NEG)] * pad
    return jnp.concatenate(rows, axis=0)


def _peer_route_kernel(x_ref, wq_ref, keys_ref, as_ref, b_ref, thr_ref):
    x = x_ref[...].astype(BF16)
    half = PEER_DK // 2
    for h in range(PEER_HEADS):
        qh = _dot(x, wq_ref[:, h * PEER_DK:(h + 1) * PEER_DK]).astype(BF16)
        s1 = _nt_dot(keys_ref[h, 0], qh[:, :half])
        s2 = _nt_dot(keys_ref[h, 1], qh[:, half:])
        t1 = _top_values(s1, PEER_TOPK)
        t2 = _top_values(s2, PEER_TOPK)
        a_un = jnp.exp(s1 - t1[0])
        b = jnp.exp(s2 - t2[0])
        a_top = [jnp.exp(v - t1[0]) for v in t1]
        b_top = [jnp.exp(v - t2[0]) for v in t2]
        z = sum(_top_values(_pair_products(a_top, b_top), PEER_TOPK))
        zinv = 1.0 / z
        thr = _top_values(_pair_products([v * zinv for v in a_top], b_top), PEER_TOPK)[-1]
        as_ref[h] = a_un * zinv
        b_ref[h] = b
        thr_ref[h] = thr


def _peer_route(x, wq, keys, tt):
    nt = x.shape[0]
    hspec = lambda n: pl.BlockSpec((PEER_HEADS, n, tt), lambda i: (0, 0, i))
    return pl.pallas_call(
        _peer_route_kernel,
        grid=(nt // tt,),
        in_specs=[pl.BlockSpec((tt, D_MODEL), lambda i: (i, 0)),
                  pl.BlockSpec(wq.shape, lambda i: (0, 0)),
                  pl.BlockSpec(keys.shape, lambda i: (0, 0, 0, 0))],
        out_specs=[hspec(PEER_NKEYS), hspec(PEER_NKEYS), hspec(1)],
        out_shape=[jax.ShapeDtypeStruct((PEER_HEADS, PEER_NKEYS, nt), F32),
                   jax.ShapeDtypeStruct((PEER_HEADS, PEER_NKEYS, nt), F32),
                   jax.ShapeDtypeStruct((PEER_HEADS, 1, nt), F32)],
        compiler_params=_cparams(("parallel",)),
        name="peer_route",
    )(x, wq, keys)


def _peer_dense_kernel(x_ref, as_ref, b_ref, thr_ref, u_ref, vt_ref, g_ref, bb_ref, y_ref,
                       gate_ref, acc_ref, *, rows_per_step):
    c = pl.program_id(1)

    @pl.when(c == 0)
    def _():
        acc_ref[...] = jnp.zeros_like(acc_ref)

    xb = x_ref[...].astype(BF16)
    act = _nt_dot(u_ref[...], xb)
    for r in range(rows_per_step):
        i1 = c * rows_per_step + r
        w = None
        for h in range(PEER_HEADS):
            p = b_ref[h] * as_ref[h, pl.ds(i1, 1), :]
            wh = jnp.where(p >= thr_ref[h], p, 0.0)
            w = wh if w is None else w + wh
        blk = slice(r * PEER_NKEYS, (r + 1) * PEER_NKEYS)
        gate_ref[blk, :] = (w * _gelu(act[blk, :])).astype(BF16)
    acc_ref[...] += _dot(vt_ref[...], gate_ref[...])

    @pl.when(c == pl.num_programs(1) - 1)
    def _():
        y = DN_ALPHA * x_ref[...] + acc_ref[...].T
        y_ref[...] = _layer_norm(y, g_ref[...], bb_ref[...])


def _peer_dense(x, a_s, b, thr, u, vt, g, bb, tt, rows_per_step=8):
    nt = x.shape[0]
    ne = rows_per_step * PEER_NKEYS
    n_exp = u.shape[0]
    hspec = lambda n: pl.BlockSpec((PEER_HEADS, n, tt), lambda i, c: (0, 0, i))
    const = lambda a: pl.BlockSpec(a.shape, lambda i, c: (0, 0))
    return pl.pallas_call(
        functools.partial(_peer_dense_kernel, rows_per_step=rows_per_step),
        grid=(nt // tt, n_exp // ne),
        in_specs=[pl.BlockSpec((tt, D_MODEL), lambda i, c: (i, 0)),
                  hspec(PEER_NKEYS), hspec(PEER_NKEYS), hspec(1),
                  pl.BlockSpec((ne, D_MODEL), lambda i, c: (c, 0)),
                  pl.BlockSpec((D_MODEL, ne), lambda i, c: (0, c)),
                  const(g), const(bb)],
        out_specs=pl.BlockSpec((tt, D_MODEL), lambda i, c: (i, 0)),
        out_shape=jax.ShapeDtypeStruct((nt, D_MODEL), F32),
        scratch_shapes=[pltpu.VMEM((ne, tt), BF16), pltpu.VMEM((D_MODEL, tt), F32)],
        compiler_params=_cparams(("parallel", "arbitrary")),
        name="peer_dense",
    )(x, a_s, b, thr, u, vt, g, bb)


def _rope_tables(pos):
    half = ROPE_DIM // 2
    inv_freq = ROPE_THETA ** (-jnp.arange(half, dtype=F32) * (2.0 / ROPE_DIM))
    ang = pos.astype(F32)[:, None] * inv_freq[None, :]
    cos, sin = jnp.cos(ang), jnp.sin(ang)
    n = pos.shape[0]
    ones = jnp.ones((n, HEAD_DIM - ROPE_DIM), F32)
    zeros = jnp.zeros((n, HEAD_DIM - ROPE_DIM), F32)
    zh = jnp.zeros((n, half), F32)
    c = jnp.concatenate([cos, cos, ones], axis=1)
    sa = jnp.concatenate([zh, sin, zeros], axis=1)
    sb = jnp.concatenate([-sin, zh, zeros], axis=1)
    two = lambda t: jnp.concatenate([t, t], axis=1)
    return two(c), two(sa), two(sb)


def _overlap(n_cmp_pad, n_sel_pad, n_cmp, n_sel):
    c0 = (CMP_STRIDE * jnp.arange(n_cmp_pad))[:, None]
    s0 = (SEL_BLOCK * jnp.arange(n_sel_pad))[None, :]
    ov = jnp.clip(jnp.minimum(c0 + CMP_BLOCK, s0 + SEL_BLOCK) - jnp.maximum(c0, s0), 0)
    ov = ov.astype(F32) * (1.0 / CMP_BLOCK)
    valid = (jnp.arange(n_cmp_pad) < n_cmp)[:, None] & (jnp.arange(n_sel_pad) < n_sel)[None, :]
    return jnp.where(valid, ov, 0.0).astype(BF16)


def _pad_to(n, m):
    return -(-n // m) * m


def _largest_divisor(n, cap):
    return max(d for d in range(1, cap + 1) if n % d == 0)


def _head_masked(t, n_heads):
    eye = jnp.eye(n_heads, dtype=t.dtype)
    out = t[..., :, :, None, :] * eye[:, None, :, None]
    return out.reshape(t.shape[:-1] + (n_heads * HEAD_DIM,))


def _layer(l, xcat, geom, cache_sb, cache_nsa, state_win, page_table, tables, W):
    B, L, Bd, Lq, n_pages = geom
    past = n_pages * PAGE_SIZE
    Np, Ns = B * L, Bd * Lq
    NT = xcat.shape[0]
    cos, sa, sb = tables

    sbq, sbkv, nq, nkv, win, gab, gates = _proj(xcat, W["w_in"][l], cos, sa, sb)
    sl_p = lambda a: a[:Np]
    sl_s = lambda a: a[Np:Np + Ns]

    new_sb_p = sl_p(sbkv).reshape(B, L, 2, SB_HEADS, HEAD_DIM)
    new_sb_s = sl_s(sbkv).reshape(Bd, Lq, 2, SB_HEADS, HEAD_DIM)
    new_nsa_p = sl_p(nkv).reshape(B, L, 4, NSA_KV_HEADS, HEAD_DIM)
    new_nsa_s = sl_s(nkv).reshape(Bd, Lq, 4, NSA_KV_HEADS, HEAD_DIM)
    win_p = sl_p(win).reshape(B, L, 2, NSA_KV_HEADS, HEAD_DIM)
    win_all_s = jnp.concatenate(
        [state_win[l], sl_s(win).reshape(Bd, Lq, 2, NSA_KV_HEADS, HEAD_DIM)], axis=1)

    heads_first = lambda t: t.transpose(0, 2, 1, 3).reshape(B * SB_HEADS, L, HEAD_DIM)
    o = _sb_prompt(heads_first(sl_p(sbq).reshape(B, L, SB_HEADS, HEAD_DIM)),
                   heads_first(new_sb_p[:, :, 0]), heads_first(new_sb_p[:, :, 1]))
    o_sb_p = o.reshape(B, SB_HEADS, L, HEAD_DIM).transpose(0, 2, 1, 3).reshape(Np, SB_WIDTH)

    qd = sl_s(sbq).reshape(Bd, Lq, SB_HEADS, HEAD_DIM).transpose(0, 2, 1, 3)
    qm = _head_masked(qd, SB_HEADS).reshape(Bd, SB_HEADS * Lq, SB_WIDTH)
    pad_rows = lambda t: jnp.pad(t, ((0, 0), (0, PAGE_SIZE - Lq), (0, 0)))
    kv_s = sl_s(sbkv).reshape(Bd, Lq, 2 * SB_WIDTH)
    o = _sb_decode(page_table, qm, pad_rows(kv_s[:, :, :SB_WIDTH]), pad_rows(kv_s[:, :, SB_WIDTH:]),
                   cache_sb[l].reshape(-1, PAGE_SIZE, 2 * SB_WIDTH), Lq)
    o = o.reshape(Bd, SB_HEADS, Lq, SB_HEADS, HEAD_DIM)
    o = jnp.einsum("bhqhd->bqhd", o)
    o_sb_s = o.reshape(Ns, SB_WIDTH)

    cw = W["cmp"][l]
    ab = _cmp_partial(nkv.reshape(NT // PAGE_SIZE, PAGE_SIZE, 4 * LANES), *cw[:4],
                      pg=_largest_divisor(NT // PAGE_SIZE, 33))
    ident = jnp.arange(Np // PAGE_SIZE, dtype=jnp.int32).reshape(B, L // PAGE_SIZE)
    ck_p, cv_p = _cmp_finish(ident, ab.reshape(NT // PAGE_SIZE, -1, 1024), cw[4])
    cache_rows = cache_nsa[l].reshape(-1, PAGE_SIZE, 4 * LANES)
    ab = _cmp_partial(cache_rows, *cw[:4], pg=_largest_divisor(cache_rows.shape[0], 32))
    ck_s, cv_s = _cmp_finish(page_table, ab.reshape(cache_rows.shape[0], -1, 1024), cw[4])

    n_cmp_p = (L - CMP_BLOCK) // CMP_STRIDE + 1
    n_sel_p = -(-L // SEL_BLOCK)
    ov_p = _overlap(ck_p.shape[1], _pad_to(n_sel_p, LANES), n_cmp_p, n_sel_p)
    qg = sl_p(nq).reshape(B, L, NSA_KV_HEADS, NSA_GROUP, HEAD_DIM).transpose(0, 2, 3, 1, 4)
    qm = _head_masked(qg.transpose(0, 2, 1, 3, 4), NSA_KV_HEADS).transpose(0, 2, 1, 3, 4)
    gt = sl_p(gates)[:, :N_GATE_COLS].reshape(B, L, NSA_KV_HEADS, NSA_GROUP, 3).transpose(0, 2, 3, 1, 4)
    o = _nsa_prompt(qm, ck_p, cv_p, sl_p(nkv).reshape(B, L, 4 * LANES),
                    sl_p(win).reshape(B, L, 2 * LANES), gt, ov_p, min(SEL_TOPN, n_sel_p))
    o = o.reshape(B, NSA_KV_HEADS, NSA_GROUP, L, NSA_KV_HEADS, HEAD_DIM)
    o = jnp.einsum("bkglkd->blkgd", o)
    o_nsa_p = o.reshape(Np, NSA_WIDTH)

    Lk = past + Lq
    n_cmp_s = (Lk - CMP_BLOCK) // CMP_STRIDE + 1
    n_sel_s = -(-Lk // SEL_BLOCK)
    ov_s = _overlap(ck_s.shape[1], _pad_to(n_sel_s, LANES), n_cmp_s, n_sel_s)
    qg = sl_s(nq).reshape(Bd, Lq, NSA_KV_HEADS, NSA_GROUP, HEAD_DIM).transpose(0, 3, 2, 1, 4)
    qm = _head_masked(qg, NSA_KV_HEADS).transpose(0, 2, 1, 3, 4).reshape(Bd, NSA_HEADS * Lq, LANES)
    gt = sl_s(gates)[:, :N_GATE_COLS].reshape(Bd, Lq, NSA_KV_HEADS, NSA_GROUP, 3)
    gt = gt.transpose(0, 2, 3, 1, 4).reshape(Bd, NSA_HEADS * Lq, 3)
    nkv_s = sl_s(nkv).reshape(Bd, Lq, 4 * LANES)
    wrows = win_all_s.reshape(Bd, -1, 2 * LANES)
    wpad = _pad_to(wrows.shape[1], PAGE_SIZE) - wrows.shape[1]
    wrows = jnp.pad(wrows, ((0, 0), (0, wpad), (0, 0)))
    o = _nsa_decode(page_table, qm, ck_s, cv_s,
                    pad_rows(nkv_s[:, :, 2 * LANES:3 * LANES]), pad_rows(nkv_s[:, :, 3 * LANES:]),
                    cache_rows, wrows[:, :, :LANES], wrows[:, :, LANES:], gt, ov_s,
                    Lq, min(SEL_TOPN, n_sel_s))
    o = o.reshape(Bd, NSA_KV_HEADS, NSA_GROUP, Lq, NSA_KV_HEADS, HEAD_DIM)
    o = jnp.einsum("bkgqkd->bqkgd", o)
    o_nsa_s = o.reshape(Ns, NSA_WIDTH)

    tail = jnp.zeros((NT - Np - Ns, SB_WIDTH), F32)
    o_sb = jnp.concatenate([o_sb_p, o_sb_s, tail], axis=0)
    o_nsa = jnp.concatenate([o_nsa_p, o_nsa_s, tail], axis=0)
    x1 = _merge(xcat, o_sb, o_nsa, gab, W["w_br_a"][l], W["w_br_b"][l], W["w_out"][l],
                W["ln1_g"][l], W["ln1_b"][l])
    tt = 512 if NT % 512 == 0 else 256
    a_s, b, thr = _peer_route(x1, W["peer_wq"][l], W["peer_keys"][l], tt)
    x2 = _peer_dense(x1, a_s, b, thr, W["peer_u"][l], W["peer_vt"][l],
                     W["ln2_g"][l], W["ln2_b"][l], tt)

    keep_s = state_win.shape[2]
    keep_p = min(WINDOW, L)
    return x2, (new_sb_p, new_sb_s, new_nsa_p, new_nsa_s, win_p[:, L - keep_p:],
                win_all_s[:, win_all_s.shape[1] - keep_s:])


def _prep_weights(w_in, w_cmp1, w_cmp2, cmp_pe, w_br_a, w_br_b, w_out, ln1_g, ln1_b,
                  peer_wq, peer_keys, peer_u, peer_v, ln2_g, ln2_b):
    depth = w_in.shape[0]
    g0 = 2816
    w_in_p = jnp.concatenate(
        [w_in[:, :, :g0], w_in[:, :, g0 + N_GATE_COLS:], w_in[:, :, g0:g0 + N_GATE_COLS],
         jnp.zeros((depth, D_MODEL, LANES - N_GATE_COLS), w_in.dtype)], axis=2).astype(BF16)
    eye = jnp.eye(NSA_KV_HEADS, dtype=F32)
    cmp = []
    for l in range(depth):
        w1 = w_cmp1[l]
        half = lambda s: jnp.einsum("tjdh,gk->tjgdkh", w1[:, s], eye).reshape(
            2, CMP_STRIDE * LANES, 2 * LANES).astype(BF16)
        pe = cmp_pe[l]
        pe_half = lambda s: jnp.broadcast_to(
            pe[:, s, None, :], (2, CMP_STRIDE, NSA_KV_HEADS, HEAD_DIM)).reshape(2, 1, CMP_STRIDE * LANES)
        w2big = jnp.einsum("thd,gk->tghkd", w_cmp2[l], eye).reshape(2, 2 * LANES, LANES).astype(BF16)
        lo, hi = slice(0, CMP_STRIDE), slice(CMP_STRIDE, CMP_BLOCK)
        cmp.append((half(lo), half(hi), pe_half(lo), pe_half(hi), w2big))
    return {
        "w_in": w_in_p, "cmp": cmp,
        "w_br_a": w_br_a.astype(BF16), "w_br_b": w_br_b.astype(BF16), "w_out": w_out.astype(BF16),
        "ln1_g": ln1_g[:, None, :], "ln1_b": ln1_b[:, None, :],
        "ln2_g": ln2_g[:, None, :], "ln2_b": ln2_b[:, None, :],
        "peer_wq": peer_wq.astype(BF16), "peer_keys": peer_keys.astype(BF16),
        "peer_u": peer_u.astype(BF16), "peer_vt": jnp.swapaxes(peer_v, 1, 2).astype(BF16),
    }


def kernel(x_prompt, x_sample, cache_sb, cache_nsa, state_win, page_table, w_in, w_cmp1, w_cmp2,
           cmp_pe, w_br_a, w_br_b, w_out, ln1_g, ln1_b, peer_wq, peer_keys, peer_u, peer_v,
           ln2_g, ln2_b):
    B, L, D = x_prompt.shape
    Bd, Lq, _ = x_sample.shape
    n_pages = page_table.shape[1]
    past = n_pages * PAGE_SIZE
    Np, Ns = B * L, Bd * Lq
    NT = _pad_to(Np + Ns, 512)
    assert L % 256 == 0 and state_win.shape[2] == WINDOW and past >= WINDOW and Lq == SUBLANES

    W = _prep_weights(w_in, w_cmp1, w_cmp2, cmp_pe, w_br_a, w_br_b, w_out, ln1_g, ln1_b,
                      peer_wq, peer_keys, peer_u, peer_v, ln2_g, ln2_b)
    pos = jnp.concatenate([jnp.tile(jnp.arange(L), B), jnp.tile(past + jnp.arange(Lq), Bd),
                           jnp.zeros((NT - Np - Ns,), jnp.int32)])
    tables = _rope_tables(pos)
    xcat = jnp.concatenate([x_prompt.reshape(Np, D), x_sample.reshape(Ns, D),
                            jnp.zeros((NT - Np - Ns, D), x_prompt.dtype)], axis=0)
    geom = (B, L, Bd, Lq, n_pages)
    outs = []
    for l in range(w_in.shape[0]):
        xcat, leaves = _layer(l, xcat, geom, cache_sb, cache_nsa, state_win, page_table, tables, W)
        outs.append(leaves)
    stack = lambda k: jnp.stack([o[k] for o in outs])
    return (xcat[:Np].reshape(B, L, D), xcat[Np:Np + Ns].reshape(Bd, Lq, D),
            stack(0), stack(1), stack(2), stack(3), stack(4), stack(5))
```

```python
import functools
import math

import jax
import jax.numpy as jnp
from jax import lax
from jax.experimental import pallas as pl
from jax.experimental.pallas import tpu as pltpu

F32 = jnp.float32
BF16 = jnp.bfloat16

D_MODEL = 1024
PAGE_SIZE = 128
HEAD_DIM = 64
SB_HEADS = 8
NSA_HEADS = 8
NSA_KV_HEADS = 2
NSA_GROUP = NSA_HEADS // NSA_KV_HEADS
SB_WIDTH = SB_HEADS * HEAD_DIM
NSA_WIDTH = NSA_HEADS * HEAD_DIM
KV_WIDTH = NSA_KV_HEADS * HEAD_DIM
ROPE_DIM = HEAD_DIM // 4
ROPE_THETA = 500000.0
CMP_BLOCK = 32
CMP_STRIDE = 16
CMP_HIDDEN = 2 * HEAD_DIM
SEL_BLOCK = 64
SEL_TOPN = 16
SEL_FORCE = 1e4
WINDOW = 512
PEER_HEADS = 8
PEER_NKEYS = 128
PEER_DK = 256
PEER_TOPK = 16
DEPTH = 2
DN_ALPHA = (2 * DEPTH) ** 0.25
LN_EPS = 1e-5
NEG = -1e30
BELOW_NEG = -3e38
ATT_SCALE = HEAD_DIM ** -0.5
SB_UNDERFLOW = -104.0

LANES = 128
SUBLANES = 8
VMEM_LIMIT = 56 * 1024 * 1024

N_GATE_COLS = 3 * NSA_HEADS
PROJ_COLS = 2816 + 2 * D_MODEL + LANES


def _cparams(sem):
    return pltpu.CompilerParams(dimension_semantics=sem, vmem_limit_bytes=VMEM_LIMIT)


def _nt_dot(a, b):
    return lax.dot_general(a, b, (((1,), (1,)), ((), ())), preferred_element_type=F32)


def _dot(a, b):
    return jnp.dot(a, b, preferred_element_type=F32)


def _gelu(x):
    c = math.sqrt(2.0 / math.pi)
    return 0.5 * x * (1.0 + jnp.tanh(c * (x + 0.044715 * (x * x * x))))


def _sigmoid(x):
    return 1.0 / (1.0 + jnp.exp(-x))


def _softplus(z):
    return jnp.maximum(z, 0.0) + jnp.log1p(jnp.exp(-jnp.abs(z)))


def _layer_norm(y, g, b):
    mu = jnp.mean(y, axis=-1, keepdims=True)
    d = y - mu
    var = jnp.mean(d * d, axis=-1, keepdims=True)
    return d * lax.rsqrt(var + LN_EPS) * g + b


def _proj_kernel(x_ref, w_ref, cos_ref, sa_ref, sb_ref,
                 sbq_ref, sbkv_ref, nq_ref, nkv_ref, win_ref, gab_ref, gates_ref):
    x = x_ref[...].astype(BF16)
    cos, sa, sb = cos_ref[...], sa_ref[...], sb_ref[...]

    def mm(c0, c1):
        return _dot(x, w_ref[:, c0:c1])

    def rope(t):
        half = ROPE_DIM // 2
        return t * cos + pltpu.roll(t, half, 1) * sa + pltpu.roll(t, LANES - half, 1) * sb

    sbq_ref[...] = mm(0, 512)
    sbkv_ref[...] = mm(512, 1536)
    hq = mm(1536, 2048)
    for i in range(4):
        nq_ref[:, i * LANES:(i + 1) * LANES] = rope(hq[:, i * LANES:(i + 1) * LANES])
    hk = mm(2048, 2560)
    nkv_ref[:, 0:128] = rope(hk[:, 0:128])
    nkv_ref[:, 128:256] = hk[:, 128:256]
    nkv_ref[:, 256:384] = rope(hk[:, 256:384])
    nkv_ref[:, 384:512] = hk[:, 384:512]
    hw = mm(2560, 2816)
    win_ref[:, 0:128] = rope(hw[:, 0:128])
    win_ref[:, 128:256] = hw[:, 128:256]
    gab_ref[:, 0:1024] = _sigmoid(mm(2816, 3840))
    gab_ref[:, 1024:2048] = _sigmoid(mm(3840, 4864))
    gates_ref[...] = _sigmoid(mm(4864, 4992))


def _proj(x, w, cos, sa, sb, tm=256):
    nt = x.shape[0]
    row = lambda n: pl.BlockSpec((tm, n), lambda i: (i, 0))
    outs = [(512, F32), (1024, F32), (512, F32), (512, F32), (256, F32), (2048, F32), (128, F32)]
    return pl.pallas_call(
        _proj_kernel,
        grid=(nt // tm,),
        in_specs=[row(D_MODEL), pl.BlockSpec((D_MODEL, PROJ_COLS), lambda i: (0, 0)),
                  row(LANES), row(LANES), row(LANES)],
        out_specs=[row(n) for n, _ in outs],
        out_shape=[jax.ShapeDtypeStruct((nt, n), dt) for n, dt in outs],
        compiler_params=_cparams(("parallel",)),
        name="in_proj",
    )(x, w, cos, sa, sb)


def _sb_block(q, k, v, tri, vis, carry):
    z = _nt_dot(q, k)
    sp = _softplus(z)
    lk = -sp if vis is None else jnp.where(vis, -sp, 0.0)
    hi = lk.astype(BF16)
    lo = (lk - hi.astype(F32)).astype(BF16)
    between = _dot(hi, tri) + _dot(lo, tri) + carry
    w = jnp.exp(z - sp + between)
    if vis is not None:
        w = jnp.where(vis, w, 0.0)
    return _dot(w.astype(BF16), v), carry + jnp.sum(lk, axis=1, keepdims=True)


def _sb_prompt_kernel(q_ref, k_ref, v_ref, tri_ref, o_ref, *, tq):
    i = pl.program_id(2)
    qf = q_ref[...] * ATT_SCALE
    tri = tri_ref[...]
    lane = lax.broadcasted_iota(jnp.int32, (1, LANES), 1)
    qpos = i * tq + lax.broadcasted_iota(jnp.int32, (tq, 1), 0)
    kio = lax.broadcasted_iota(jnp.int32, (1, tq), 1)

    def cond(s):
        return (s[0] <= i) & (s[3] > SB_UNDERFLOW)

    out = jnp.zeros((tq, LANES), F32)
    for hh in range(LANES // HEAD_DIM):
        in_head = (lane >= hh * HEAD_DIM) & (lane < (hh + 1) * HEAD_DIM)
        q = jnp.where(in_head, qf, 0.0).astype(BF16)

        def body(s, q=q):
            jj, acc, car, _ = s
            start = pl.multiple_of((i - jj) * tq, tq)
            k = k_ref[pl.ds(start, tq), :].astype(BF16)
            v = v_ref[pl.ds(start, tq), :].astype(BF16)
            contrib, car = _sb_block(q, k, v, tri, (start + kio) < qpos, car)
            return jj + 1, acc + contrib, car, jnp.max(car)

        init = (jnp.int32(0), jnp.zeros((tq, LANES), F32), jnp.zeros((tq, 1), F32),
                jnp.float32(0.0))
        acc = lax.while_loop(cond, body, init)[1]
        out = jnp.where(in_head, acc, out)
    o_ref[...] = out


def _strict_lower(n):
    r = lax.broadcasted_iota(jnp.int32, (n, n), 0)
    c = lax.broadcasted_iota(jnp.int32, (n, n), 1)
    return (r > c).astype(BF16)


def _sb_prompt(sbq, sbkv, B, L, tq=256):
    tq = min(tq, L)
    nq = L // tq
    pairs = SB_WIDTH // LANES
    blk = lambda b, h, i: (b * nq + i, h)
    return pl.pallas_call(
        functools.partial(_sb_prompt_kernel, tq=tq),
        grid=(B, pairs, nq),
        in_specs=[pl.BlockSpec((tq, LANES), blk),
                  pl.BlockSpec((L, LANES), lambda b, h, i: (b, h)),
                  pl.BlockSpec((L, LANES), lambda b, h, i: (b, pairs + h)),
                  pl.BlockSpec((tq, tq), lambda b, h, i: (0, 0))],
        out_specs=pl.BlockSpec((tq, LANES), blk),
        out_shape=jax.ShapeDtypeStruct((B * L, SB_WIDTH), F32),
        compiler_params=_cparams(("parallel", "parallel", "arbitrary")),
        name="sb_prompt",
    )(sbq, sbkv, sbkv, _strict_lower(tq))


def _sb_decode_kernel(pt_ref, q_ref, kn_ref, vn_ref, cache_ref, tri_ref, o_ref, buf_ref, sem_ref,
                      *, layer, n_pages, lq):
    b = pl.program_id(0)
    rows = SB_HEADS * lq
    past = n_pages * PAGE_SIZE
    tri = tri_ref[...]
    q = (q_ref[0] * ATT_SCALE).astype(BF16)
    qpos = past + lax.broadcasted_iota(jnp.int32, (rows, 1), 0) % lq
    kio = lax.broadcasted_iota(jnp.int32, (1, PAGE_SIZE), 1)

    def page_copy(p, slot):
        return pltpu.make_async_copy(cache_ref.at[layer, pt_ref[b, p]], buf_ref.at[slot],
                                     sem_ref.at[slot])

    page_copy(n_pages - 1, 0).start()
    acc, car = _sb_block(q, kn_ref[0].astype(BF16), vn_ref[0].astype(BF16), tri,
                         (past + kio) < qpos, jnp.zeros((rows, 1), F32))

    def cond(s):
        return (s[0] < n_pages) & (s[3] > SB_UNDERFLOW)

    def body(s):
        jj, acc, car, _ = s
        slot = jj % 2
        p = n_pages - 1 - jj
        page_copy(p, slot).wait()

        @pl.when(jj + 1 < n_pages)
        def _():
            page_copy(p - 1, 1 - slot).start()

        k = buf_ref[slot, :, 0:SB_WIDTH].astype(BF16)
        v = buf_ref[slot, :, SB_WIDTH:2 * SB_WIDTH].astype(BF16)
        contrib, car = _sb_block(q, k, v, tri, None, car)
        return jj + 1, acc + contrib, car, jnp.max(car)

    jj, acc, _, _ = lax.while_loop(cond, body, (jnp.int32(0), acc, car, jnp.max(car)))

    @pl.when(jj < n_pages)
    def _():
        page_copy(n_pages - 1 - jj, jj % 2).wait()

    o_ref[0] = acc


def _sb_decode(page_table, qm, k_new, v_new, cache, layer, lq):
    bd, n_pages = page_table.shape
    rows = SB_HEADS * lq
    per_b = lambda r, c: pl.BlockSpec((1, r, c), lambda b, pt: (b, 0, 0))
    grid_spec = pltpu.PrefetchScalarGridSpec(
        num_scalar_prefetch=1,
        grid=(bd,),
        in_specs=[per_b(rows, SB_WIDTH), per_b(PAGE_SIZE, SB_WIDTH), per_b(PAGE_SIZE, SB_WIDTH),
                  pl.BlockSpec(memory_space=pl.ANY),
                  pl.BlockSpec((PAGE_SIZE, PAGE_SIZE), lambda b, pt: (0, 0))],
        out_specs=per_b(rows, SB_WIDTH),
        scratch_shapes=[pltpu.VMEM((2, PAGE_SIZE, 2 * SB_WIDTH), F32),
                        pltpu.SemaphoreType.DMA((2,))],
    )
    return pl.pallas_call(
        functools.partial(_sb_decode_kernel, layer=layer, n_pages=n_pages, lq=lq),
        grid_spec=grid_spec,
        out_shape=jax.ShapeDtypeStruct((bd, rows, SB_WIDTH), F32),
        compiler_params=_cparams(("arbitrary",)),
        name="sb_decode",
    )(page_table, qm, k_new, v_new, cache, _strict_lower(PAGE_SIZE))


def _cmp_partial_kernel(rk_ref, rv_ref, wa_ref, wb_ref, pea_ref, peb_ref, o_ref, xk_ref, xv_ref,
                        *, pg):
    nchunk = PAGE_SIZE // CMP_STRIDE

    def gather(p, c):
        dst = pl.ds(pl.multiple_of(p * nchunk, nchunk), nchunk)
        for j in range(CMP_STRIDE):
            src = pl.ds(j, nchunk, stride=CMP_STRIDE)
            xk_ref[dst, j * LANES:(j + 1) * LANES] = rk_ref[p, src, :]
            xv_ref[dst, j * LANES:(j + 1) * LANES] = rv_ref[p, src, :]
        return c

    lax.fori_loop(0, pg, gather, 0)
    for t, x_ref in enumerate((xk_ref, xv_ref)):
        x = x_ref[...]
        a = _dot((x + pea_ref[t]).astype(BF16), wa_ref[t])
        b = _dot((x + peb_ref[t]).astype(BF16), wb_ref[t])
        o_ref[:, t * 512:t * 512 + 256] = a
        o_ref[:, t * 512 + 256:(t + 1) * 512] = b


def _cmp_partial(rows, wa, wb, pea, peb, pg, n_pages, page0=0):
    nchunk = PAGE_SIZE // CMP_STRIDE
    kdim = CMP_STRIDE * LANES
    blk0 = page0 // pg
    const3 = lambda s: pl.BlockSpec(s, lambda i: (0, 0, 0))
    return pl.pallas_call(
        functools.partial(_cmp_partial_kernel, pg=pg),
        grid=(n_pages // pg,),
        in_specs=[pl.BlockSpec((pg, PAGE_SIZE, LANES), lambda i: (blk0 + i, 0, 0)),
                  pl.BlockSpec((pg, PAGE_SIZE, LANES), lambda i: (blk0 + i, 0, 1)),
                  const3((2, kdim, 2 * LANES)), const3((2, kdim, 2 * LANES)),
                  const3((2, 1, kdim)), const3((2, 1, kdim))],
        out_specs=pl.BlockSpec((pg * nchunk, 1024), lambda i: (i, 0)),
        out_shape=jax.ShapeDtypeStruct((n_pages * nchunk, 1024), F32),
        scratch_shapes=[pltpu.VMEM((pg * nchunk, kdim), F32), pltpu.VMEM((pg * nchunk, kdim), F32)],
        compiler_params=_cparams(("parallel",)),
        name="cmp_partial",
    )(rows, rows, wa, wb, pea, peb)


def _cmp_finish_kernel(pt_ref, ab_ref, w2_ref, ck_ref, cv_ref, buf_ref, sem_ref, *, pp):
    b = pl.program_id(0)
    nchunk = PAGE_SIZE // CMP_STRIDE

    def copy(j):
        return pltpu.make_async_copy(ab_ref.at[pt_ref[b, j]], buf_ref.at[j], sem_ref.at[j])

    for j in range(pp):
        copy(j).start()
    for j in range(pp):
        copy(j).wait()

    n = pp * nchunk
    x = buf_ref[...].reshape(n, 1024)
    last = lax.broadcasted_iota(jnp.int32, (n, 1), 0) == n - 1
    for t, o_ref in enumerate((ck_ref, cv_ref)):
        a = x[:, t * 512:t * 512 + 256]
        bnext = pltpu.roll(x[:, t * 512 + 256:(t + 1) * 512], n - 1, 0)
        hid = _gelu(a + jnp.where(last, 0.0, bnext))
        o_ref[0] = _dot(hid.astype(BF16), w2_ref[t])


def _cmp_finish(page_table, ab, w2big):
    nb, pp = page_table.shape
    nchunk = PAGE_SIZE // CMP_STRIDE
    n = pp * nchunk
    grid_spec = pltpu.PrefetchScalarGridSpec(
        num_scalar_prefetch=1,
        grid=(nb,),
        in_specs=[pl.BlockSpec(memory_space=pl.ANY),
                  pl.BlockSpec((2, 2 * LANES, LANES), lambda b, pt: (0, 0, 0))],
        out_specs=[pl.BlockSpec((1, n, LANES), lambda b, pt: (b, 0, 0))] * 2,
        scratch_shapes=[pltpu.VMEM((pp, nchunk, 1024), F32), pltpu.SemaphoreType.DMA((pp,))],
    )
    return pl.pallas_call(
        functools.partial(_cmp_finish_kernel, pp=pp),
        grid_spec=grid_spec,
        out_shape=[jax.ShapeDtypeStruct((nb, n, LANES), F32)] * 2,
        compiler_params=_cparams(("arbitrary",)),
        name="cmp_finish",
    )(page_table, ab, w2big)


def _topk_mask(score, n_top, axis):
    idx_f = lax.broadcasted_iota(jnp.int32, score.shape, axis).astype(F32)
    sel = jnp.zeros(score.shape, F32)
    work = score
    for _ in range(n_top):
        m = jnp.max(work, axis=axis, keepdims=True)
        first = jnp.min(jnp.where(work == m, idx_f, float(score.shape[axis])), axis=axis,
                        keepdims=True)
        hit = idx_f == first
        sel = jnp.where(hit & (m > 0.5 * NEG), 1.0, sel)
        work = jnp.where(hit, BELOW_NEG, work)
    return sel


def _cmp_branch(q, ck, cv, qpos):
    sc = _nt_dot(q, ck.astype(BF16))
    nio = lax.broadcasted_iota(jnp.int32, (1, sc.shape[1]), 1)
    cvis = (CMP_STRIDE * nio + (CMP_BLOCK - 1)) <= qpos
    scm = jnp.where(cvis, sc, NEG)
    e = jnp.exp(scm - jnp.max(scm, axis=-1, keepdims=True))
    pc = jnp.where(cvis, e / jnp.sum(e, axis=-1, keepdims=True), 0.0).astype(BF16)
    return _dot(pc, cv.astype(BF16)), pc


def _select_blocks(imp, qpos, n_top, axis):
    shape = (1, imp.shape[1]) if axis == 1 else (imp.shape[0], 1)
    sio = lax.broadcasted_iota(jnp.int32, shape, axis)
    cur = qpos // SEL_BLOCK
    forced = (sio == 0) | (sio == cur) | (sio == cur - 1)
    imp = jnp.where(sio <= cur, imp + jnp.where(forced, SEL_FORCE, 0.0), NEG)
    return _topk_mask(imp, n_top, axis)


def _softmax_step(state, s, mask, v):
    m, l, acc = state
    sm = jnp.where(mask, s, NEG)
    m_new = jnp.maximum(m, jnp.max(sm, axis=-1, keepdims=True))
    alpha = jnp.exp(m - m_new)
    p = jnp.where(mask, jnp.exp(sm - m_new), 0.0)
    return (m_new, alpha * l + jnp.sum(p, axis=-1, keepdims=True),
            alpha * acc + _dot(p.astype(BF16), v))


def _softmax_init(r):
    return (jnp.full((r, 1), NEG, F32), jnp.zeros((r, 1), F32), jnp.zeros((r, LANES), F32))


def _nsa_prompt_kernel(q_ref, ck_ref, cv_ref, ks_ref, vs_ref, kw_ref, vw_ref, g_ref, ovt_ref,
                       o_ref, *, tq, n_top):
    i = pl.program_id(2)
    r = NSA_GROUP * tq
    q = (q_ref[0, 0].reshape(r, LANES) * ATT_SCALE).astype(BF16)
    qpos = i * tq + lax.broadcasted_iota(jnp.int32, (r, 1), 0) % tq

    o_c, pc = _cmp_branch(q, ck_ref[0], cv_ref[0], qpos)
    ovt = ovt_ref[...]
    imp_t = _nt_dot(ovt, pc[0:tq])
    for g in range(1, NSA_GROUP):
        imp_t = imp_t + _nt_dot(ovt, pc[g * tq:(g + 1) * tq])
    qpos_t = i * tq + lax.broadcasted_iota(jnp.int32, (1, tq), 1)
    sel = _select_blocks(imp_t, qpos_t, n_top, 0).T.astype(BF16)
    ns = sel.shape[1]

    tk = 2 * tq
    per = tk // SEL_BLOCK
    sio = lax.broadcasted_iota(jnp.int32, (ns, tk), 0)
    kio_s = lax.broadcasted_iota(jnp.int32, (ns, tk), 1) // SEL_BLOCK
    kio = lax.broadcasted_iota(jnp.int32, (1, tk), 1)

    def sel_body(j, state):
        start = pl.multiple_of(j * tk, tk)
        k = ks_ref[0, pl.ds(start, tk), :].astype(BF16)
        v = vs_ref[0, pl.ds(start, tk), :].astype(BF16)
        expand = (sio == kio_s + j * per).astype(BF16)
        picked = _dot(sel, expand)
        picked = jnp.concatenate([picked] * NSA_GROUP, axis=0)
        mask = (picked > 0.5) & ((start + kio) <= qpos)
        return _softmax_step(state, _nt_dot(q, k), mask, v)

    n_sel_steps = (i * tq + tq - 1) // tk + 1
    m_s, l_s, acc_s = lax.fori_loop(0, n_sel_steps, sel_body, _softmax_init(r))

    wio = lax.broadcasted_iota(jnp.int32, (1, tq), 1)

    def win_body(j, state):
        start = pl.multiple_of(j * tq, tq)
        k = kw_ref[0, pl.ds(start, tq), :].astype(BF16)
        v = vw_ref[0, pl.ds(start, tq), :].astype(BF16)
        rel = qpos - (start + wio)
        mask = (rel >= 0) & (rel < WINDOW)
        return _softmax_step(state, _nt_dot(q, k), mask, v)

    m_w, l_w, acc_w = lax.fori_loop(jnp.maximum(i - WINDOW // tq, 0), i + 1, win_body,
                                    _softmax_init(r))

    g = g_ref[0, 0].reshape(r, 3)
    o = o_c * g[:, 0:1] + (acc_s / l_s) * g[:, 1:2] + (acc_w / l_w) * g[:, 2:3]
    o_ref[0, 0] = o.reshape(NSA_GROUP, tq, LANES)


def _nsa_prompt(qm, ck, cv, nkv, win, gates, ov, n_top, tq=128):
    B, _, _, L, _ = qm.shape
    nc = ck.shape[1]
    seq = lambda lane_blk: pl.BlockSpec((1, L, LANES), lambda b, h, i: (b, 0, lane_blk))
    qspec = lambda last: pl.BlockSpec((1, 1, NSA_GROUP, tq, last), lambda b, h, i: (b, h, 0, i, 0))
    cspec = pl.BlockSpec((1, nc, LANES), lambda b, h, i: (b, 0, 0))
    return pl.pallas_call(
        functools.partial(_nsa_prompt_kernel, tq=tq, n_top=n_top),
        grid=(B, NSA_KV_HEADS, L // tq),
        in_specs=[qspec(LANES), cspec, cspec, seq(2), seq(3), seq(0), seq(1), qspec(3),
                  pl.BlockSpec(ov.shape, lambda b, h, i: (0, 0))],
        out_specs=qspec(LANES),
        out_shape=jax.ShapeDtypeStruct(qm.shape, F32),
        compiler_params=_cparams(("parallel", "parallel", "arbitrary")),
        name="nsa_prompt",
    )(qm, ck, cv, nkv, nkv, win, win, gates, ov)


def _nsa_decode_kernel(pt_ref, q_ref, ck_ref, cv_ref, kn_ref, vn_ref, cache_ref, kw_ref, vw_ref,
                       g_ref, ov_ref, ex_ref, o_ref, buf_ref, sem_ref,
                       *, layer, n_pages, lq, n_top):
    b = pl.program_id(0)
    r = NSA_HEADS * lq
    rk = NSA_KV_HEADS * lq
    past = n_pages * PAGE_SIZE

    def page_copy(j):
        return pltpu.make_async_copy(
            cache_ref.at[layer, pt_ref[b, j], :, pl.ds(2 * LANES, 2 * LANES)],
            buf_ref.at[j], sem_ref.at[j])

    for j in range(n_pages):
        page_copy(j).start()

    q = (q_ref[0] * ATT_SCALE).astype(BF16)
    qpos = past + lax.broadcasted_iota(jnp.int32, (r, 1), 0) % lq
    kio = lax.broadcasted_iota(jnp.int32, (1, PAGE_SIZE), 1)

    o_c, pc = _cmp_branch(q, ck_ref[0], cv_ref[0], qpos)
    impg = _dot(pc, ov_ref[...])
    ns = impg.shape[1]
    impg = impg.reshape(NSA_KV_HEADS, NSA_GROUP, lq, ns)
    imp = impg[:, 0]
    for g in range(1, NSA_GROUP):
        imp = imp + impg[:, g]
    qpos1 = past + lax.broadcasted_iota(jnp.int32, (rk, 1), 0) % lq
    sel = _select_blocks(imp.reshape(rk, ns), qpos1, n_top, 1)
    nk = past + PAGE_SIZE
    picked = _dot(sel.astype(BF16), ex_ref[...])
    picked = jnp.broadcast_to(picked.reshape(NSA_KV_HEADS, 1, lq, nk),
                              (NSA_KV_HEADS, NSA_GROUP, lq, nk)).reshape(r, nk)
    kpos = lax.broadcasted_iota(jnp.int32, (1, nk), 1)
    mask = (picked > 0.5) & (kpos <= qpos)

    for j in range(n_pages):
        page_copy(j).wait()
    k_past = buf_ref[:, :, 0:LANES].reshape(past, LANES).astype(BF16)
    v_past = buf_ref[:, :, LANES:2 * LANES].reshape(past, LANES).astype(BF16)
    s = jnp.concatenate([_nt_dot(q, k_past), _nt_dot(q, kn_ref[0].astype(BF16))], axis=1)
    sm = jnp.where(mask, s, NEG)
    p = jnp.where(mask, jnp.exp(sm - jnp.max(sm, axis=-1, keepdims=True)), 0.0)
    l_s = jnp.sum(p, axis=-1, keepdims=True)
    p = p.astype(BF16)
    acc_s = _dot(p[:, :past], v_past) + _dot(p[:, past:], vn_ref[0].astype(BF16))

    state = _softmax_init(r)
    for w in range(kw_ref.shape[1] // PAGE_SIZE):
        k = kw_ref[0, w * PAGE_SIZE:(w + 1) * PAGE_SIZE, :].astype(BF16)
        v = vw_ref[0, w * PAGE_SIZE:(w + 1) * PAGE_SIZE, :].astype(BF16)
        rel = qpos - (past - WINDOW + w * PAGE_SIZE + kio)
        state = _softmax_step(state, _nt_dot(q, k), (rel >= 0) & (rel < WINDOW), v)
    _, l_w, acc_w = state
    g = g_ref[0]
    o_ref[0] = o_c * g[:, 0:1] + (acc_s / l_s) * g[:, 1:2] + (acc_w / l_w) * g[:, 2:3]


def _nsa_decode(page_table, qm, ck, cv, k_new, v_new, cache, kwin, vwin, gates, ov, ex,
                layer, lq, n_top):
    bd, n_pages = page_table.shape
    r = NSA_HEADS * lq
    per_b = lambda a: pl.BlockSpec((1,) + a.shape[1:], lambda b, pt: (b, 0, 0))
    const = lambda a: pl.BlockSpec(a.shape, lambda b, pt: (0, 0))
    grid_spec = pltpu.PrefetchScalarGridSpec(
        num_scalar_prefetch=1,
        grid=(bd,),
        in_specs=[per_b(qm), per_b(ck), per_b(cv), per_b(k_new), per_b(v_new),
                  pl.BlockSpec(memory_space=pl.ANY),
                  per_b(kwin), per_b(vwin), per_b(gates), const(ov), const(ex)],
        out_specs=pl.BlockSpec((1, r, LANES), lambda b, pt: (b, 0, 0)),
        scratch_shapes=[pltpu.VMEM((n_pages, PAGE_SIZE, 2 * LANES), F32),
                        pltpu.SemaphoreType.DMA((n_pages,))],
    )
    return pl.pallas_call(
        functools.partial(_nsa_decode_kernel, layer=layer, n_pages=n_pages, lq=lq, n_top=n_top),
        grid_spec=grid_spec,
        out_shape=jax.ShapeDtypeStruct((bd, r, LANES), F32),
        compiler_params=_cparams(("arbitrary",)),
        name="nsa_decode",
    )(page_table, qm, ck, cv, k_new, v_new, cache, kwin, vwin, gates, ov, ex)


def _merge_kernel(x_ref, oa_ref, ob_ref, gab_ref, wa_ref, wb_ref, wo_ref, g_ref, b_ref, y_ref):
    a = _dot(oa_ref[...].astype(BF16), wa_ref[...])
    b = _dot(ob_ref[...].astype(BF16), wb_ref[...])
    merged = gab_ref[:, 0:D_MODEL] * a + gab_ref[:, D_MODEL:2 * D_MODEL] * b
    y = DN_ALPHA * x_ref[...] + _dot(merged.astype(BF16), wo_ref[...])
    y_ref[...] = _layer_norm(y, g_ref[...], b_ref[...])


def _merge(x, oa, ob, gab, wa, wb, wo, g, b, tm=256):
    nt = x.shape[0]
    row = lambda n: pl.BlockSpec((tm, n), lambda i: (i, 0))
    const = lambda a: pl.BlockSpec(a.shape, lambda i: (0, 0))
    return pl.pallas_call(
        _merge_kernel,
        grid=(nt // tm,),
        in_specs=[row(D_MODEL), row(SB_WIDTH), row(NSA_WIDTH), row(2 * D_MODEL),
                  const(wa), const(wb), const(wo), const(g), const(b)],
        out_specs=row(D_MODEL),
        out_shape=jax.ShapeDtypeStruct((nt, D_MODEL), F32),
        compiler_params=_cparams(("parallel",)),
        name="merge_ln1",
    )(x, oa, ob, gab, wa, wb, wo, g, b)


def _top_values(x, k):
    n = x.shape[0]
    rio = lax.broadcasted_iota(jnp.int32, x.shape, 0).astype(F32)
    vals = []
    for _ in range(k):
        m = jnp.max(x, axis=0, keepdims=True)
        idx = jnp.min(jnp.where(x == m, rio, float(n)), axis=0, keepdims=True)
        x = jnp.where(rio == idx, BELOW_NEG, x)
        vals.append(m)
    return vals


def _pair_products(a, b):
    rows = [a[i] * b[j] for i in range(len(a)) for j in range(len(b))
            if (i + 1) * (j + 1) <= PEER_TOPK]
    pad = (-len(rows)) % SUBLANES
    rows += [jnp.full_like(rows[0], BELOW_NEG)] * pad
    return jnp.concatenate(rows, axis=0)


def _peer_route_kernel(x_ref, wq_ref, keys_ref, as_ref, b_ref, thr_ref):
    x = x_ref[...].astype(BF16)
    half = PEER_DK // 2
    for h in range(PEER_HEADS):
        qh = _dot(x, wq_ref[:, h * PEER_DK:(h + 1) * PEER_DK]).astype(BF16)
        s1 = _nt_dot(keys_ref[h, 0], qh[:, :half])
        s2 = _nt_dot(keys_ref[h, 1], qh[:, half:])
        t1 = _top_values(s1, PEER_TOPK)
        t2 = _top_values(s2, PEER_TOPK)
        a_un = jnp.exp(s1 - t1[0])
        b = jnp.exp(s2 - t2[0])
        a_top = [jnp.exp(v - t1[0]) for v in t1]
        b_top = [jnp.exp(v - t2[0]) for v in t2]
        z = sum(_top_values(_pair_products(a_top, b_top), PEER_TOPK))
        zinv = 1.0 / z
        thr = _top_values(_pair_products([v * zinv for v in a_top], b_top), PEER_TOPK)[-1]
        as_ref[h] = a_un * zinv
        b_ref[h] = b
        thr_ref[h] = thr


def _peer_route(x, wq, keys, tt):
    nt = x.shape[0]
    hspec = lambda n: pl.BlockSpec((PEER_HEADS, n, tt), lambda i: (0, 0, i))
    return pl.pallas_call(
        _peer_route_kernel,
        grid=(nt // tt,),
        in_specs=[pl.BlockSpec((tt, D_MODEL), lambda i: (i, 0)),
                  pl.BlockSpec(wq.shape, lambda i: (0, 0)),
                  pl.BlockSpec(keys.shape, lambda i: (0, 0, 0, 0))],
        out_specs=[hspec(PEER_NKEYS), hspec(PEER_NKEYS), hspec(1)],
        out_shape=[jax.ShapeDtypeStruct((PEER_HEADS, PEER_NKEYS, nt), F32),
                   jax.ShapeDtypeStruct((PEER_HEADS, PEER_NKEYS, nt), F32),
                   jax.ShapeDtypeStruct((PEER_HEADS, 1, nt), F32)],
        compiler_params=_cparams(("parallel",)),
        name="peer_route",
    )(x, wq, keys)


def _peer_dense_kernel(x_ref, as_ref, b_ref, thr_ref, u_ref, vt_ref, g_ref, bb_ref, y_ref,
                       acc_ref, *, rows_per_step, n_split):
    c = pl.program_id(1)

    @pl.when(c == 0)
    def _():
        acc_ref[...] = jnp.zeros_like(acc_ref)

    xb = x_ref[...].astype(BF16)
    rows_g = rows_per_step // n_split
    ne_g = rows_g * PEER_NKEYS
    total = None
    for s in range(n_split):
        act = _nt_dot(u_ref[s * ne_g:(s + 1) * ne_g, :], xb)
        gates = []
        for r in range(rows_g):
            i1 = c * rows_per_step + s * rows_g + r
            w = None
            for h in range(PEER_HEADS):
                p = b_ref[h] * as_ref[h, pl.ds(i1, 1), :]
                wh = jnp.where(p >= thr_ref[h], p, 0.0)
                w = wh if w is None else w + wh
            blk = slice(r * PEER_NKEYS, (r + 1) * PEER_NKEYS)
            gates.append((w * _gelu(act[blk, :])).astype(BF16))
        part = _dot(vt_ref[:, s * ne_g:(s + 1) * ne_g], jnp.concatenate(gates, axis=0))
        total = part if total is None else total + part
    acc_ref[...] += total

    @pl.when(c == pl.num_programs(1) - 1)
    def _():
        y = DN_ALPHA * x_ref[...] + acc_ref[...].T
        y_ref[...] = _layer_norm(y, g_ref[...], bb_ref[...])


def _peer_dense(x, a_s, b, thr, u, vt, g, bb, tt, rows_per_step=16, n_split=4):
    nt = x.shape[0]
    ne = rows_per_step * PEER_NKEYS
    n_exp = u.shape[0]
    hspec = lambda n: pl.BlockSpec((PEER_HEADS, n, tt), lambda i, c: (0, 0, i))
    const = lambda a: pl.BlockSpec(a.shape, lambda i, c: (0, 0))
    return pl.pallas_call(
        functools.partial(_peer_dense_kernel, rows_per_step=rows_per_step, n_split=n_split),
        grid=(nt // tt, n_exp // ne),
        in_specs=[pl.BlockSpec((tt, D_MODEL), lambda i, c: (i, 0)),
                  hspec(PEER_NKEYS), hspec(PEER_NKEYS), hspec(1),
                  pl.BlockSpec((ne, D_MODEL), lambda i, c: (c, 0)),
                  pl.BlockSpec((D_MODEL, ne), lambda i, c: (0, c)),
                  const(g), const(bb)],
        out_specs=pl.BlockSpec((tt, D_MODEL), lambda i, c: (i, 0)),
        out_shape=jax.ShapeDtypeStruct((nt, D_MODEL), F32),
        scratch_shapes=[pltpu.VMEM((D_MODEL, tt), F32)],
        compiler_params=_cparams(("parallel", "arbitrary")),
        name="peer_dense",
    )(x, a_s, b, thr, u, vt, g, bb)


def _cast_kernel(x_ref, o_ref, *, transpose):
    x = x_ref[0]
    o_ref[0] = (x.T if transpose else x).astype(BF16)


def _expert_table_bf16(w, transpose, tr=512):
    depth, n_exp, d = w.shape
    if transpose:
        out_shape, out_spec = (depth, d, n_exp), pl.BlockSpec((1, d, tr), lambda l, i: (l, 0, i))
    else:
        out_shape, out_spec = (depth, n_exp, d), pl.BlockSpec((1, tr, d), lambda l, i: (l, i, 0))
    return pl.pallas_call(
        functools.partial(_cast_kernel, transpose=transpose),
        grid=(depth, n_exp // tr),
        in_specs=[pl.BlockSpec((1, tr, d), lambda l, i: (l, i, 0))],
        out_specs=out_spec,
        out_shape=jax.ShapeDtypeStruct(out_shape, BF16),
        compiler_params=_cparams(("parallel", "parallel")),
        name="expert_table_bf16",
    )(w)


def _rope_tables(pos):
    half = ROPE_DIM // 2
    inv_freq = ROPE_THETA ** (-jnp.arange(half, dtype=F32) * (2.0 / ROPE_DIM))
    ang = pos.astype(F32)[:, None] * inv_freq[None, :]
    cos, sin = jnp.cos(ang), jnp.sin(ang)
    n = pos.shape[0]
    ones = jnp.ones((n, HEAD_DIM - ROPE_DIM), F32)
    zeros = jnp.zeros((n, HEAD_DIM - ROPE_DIM), F32)
    zh = jnp.zeros((n, half), F32)
    c = jnp.concatenate([cos, cos, ones], axis=1)
    sa = jnp.concatenate([zh, sin, zeros], axis=1)
    sb = jnp.concatenate([-sin, zh, zeros], axis=1)
    two = lambda t: jnp.concatenate([t, t], axis=1)
    return two(c), two(sa), two(sb)


def _overlap(n_cmp_pad, n_sel_pad, n_cmp, n_sel):
    c0 = (CMP_STRIDE * jnp.arange(n_cmp_pad))[:, None]
    s0 = (SEL_BLOCK * jnp.arange(n_sel_pad))[None, :]
    ov = jnp.clip(jnp.minimum(c0 + CMP_BLOCK, s0 + SEL_BLOCK) - jnp.maximum(c0, s0), 0)
    ov = ov.astype(F32) * (1.0 / CMP_BLOCK)
    valid = (jnp.arange(n_cmp_pad) < n_cmp)[:, None] & (jnp.arange(n_sel_pad) < n_sel)[None, :]
    return jnp.where(valid, ov, 0.0).astype(BF16)


def _pad_to(n, m):
    return -(-n // m) * m


def _largest_divisor(n, cap):
    return max(d for d in range(1, cap + 1) if n % d == 0)


def _head_masked(t, n_heads):
    eye = jnp.eye(n_heads, dtype=t.dtype)
    out = t[..., :, :, None, :] * eye[:, None, :, None]
    return out.reshape(t.shape[:-1] + (n_heads * HEAD_DIM,))


def _layer(l, xcat, geom, cache_sb, cache_nsa, state_win, page_table, tables, W):
    B, L, Bd, Lq, n_pages = geom
    past = n_pages * PAGE_SIZE
    Np, Ns = B * L, Bd * Lq
    NT = xcat.shape[0]
    cos, sa, sb = tables

    sbq, sbkv, nq, nkv, win, gab, gates = _proj(xcat, W["w_in"][l], cos, sa, sb)
    sl_p = lambda a: a[:Np]
    sl_s = lambda a: a[Np:Np + Ns]

    new_sb_p = sl_p(sbkv).reshape(B, L, 2, SB_HEADS, HEAD_DIM)
    new_sb_s = sl_s(sbkv).reshape(Bd, Lq, 2, SB_HEADS, HEAD_DIM)
    new_nsa_p = sl_p(nkv).reshape(B, L, 4, NSA_KV_HEADS, HEAD_DIM)
    new_nsa_s = sl_s(nkv).reshape(Bd, Lq, 4, NSA_KV_HEADS, HEAD_DIM)
    win_p = sl_p(win).reshape(B, L, 2, NSA_KV_HEADS, HEAD_DIM)
    win_all_s = jnp.concatenate(
        [state_win[l], sl_s(win).reshape(Bd, Lq, 2, NSA_KV_HEADS, HEAD_DIM)], axis=1)

    o_sb_p = _sb_prompt(sbq, sbkv, B, L)

    n_pool = cache_sb.shape[1]
    qd = sl_s(sbq).reshape(Bd, Lq, SB_HEADS, HEAD_DIM).transpose(0, 2, 1, 3)
    qm = _head_masked(qd, SB_HEADS).reshape(Bd, SB_HEADS * Lq, SB_WIDTH)
    pad_rows = lambda t: jnp.pad(t, ((0, 0), (0, PAGE_SIZE - Lq), (0, 0)))
    kv_s = sl_s(sbkv).reshape(Bd, Lq, 2 * SB_WIDTH)
    o = _sb_decode(page_table, qm, pad_rows(kv_s[:, :, :SB_WIDTH]), pad_rows(kv_s[:, :, SB_WIDTH:]),
                   cache_sb.reshape(-1, n_pool, PAGE_SIZE, 2 * SB_WIDTH), l, Lq)
    o = o.reshape(Bd, SB_HEADS, Lq, SB_HEADS, HEAD_DIM)
    o = jnp.einsum("bhqhd->bqhd", o)
    o_sb_s = o.reshape(Ns, SB_WIDTH)

    cw = W["cmp"][l]
    ab = _cmp_partial(nkv.reshape(NT // PAGE_SIZE, PAGE_SIZE, 4 * LANES), *cw[:4],
                      pg=_largest_divisor(NT // PAGE_SIZE, 33), n_pages=NT // PAGE_SIZE)
    ident = jnp.arange(Np // PAGE_SIZE, dtype=jnp.int32).reshape(B, L // PAGE_SIZE)
    ck_p, cv_p = _cmp_finish(ident, ab.reshape(NT // PAGE_SIZE, -1, 1024), cw[4])
    ab = _cmp_partial(cache_nsa.reshape(-1, PAGE_SIZE, 4 * LANES), *cw[:4],
                      pg=_largest_divisor(n_pool, 32), n_pages=n_pool, page0=l * n_pool)
    ck_s, cv_s = _cmp_finish(page_table, ab.reshape(n_pool, -1, 1024), cw[4])

    n_cmp_p = (L - CMP_BLOCK) // CMP_STRIDE + 1
    n_sel_p = -(-L // SEL_BLOCK)
    ov_p = _overlap(ck_p.shape[1], _pad_to(n_sel_p, LANES), n_cmp_p, n_sel_p).T
    qg = sl_p(nq).reshape(B, L, NSA_KV_HEADS, NSA_GROUP, HEAD_DIM).transpose(0, 2, 3, 1, 4)
    qm = _head_masked(qg.transpose(0, 2, 1, 3, 4), NSA_KV_HEADS).transpose(0, 2, 1, 3, 4)
    gt = sl_p(gates)[:, :N_GATE_COLS].reshape(B, L, NSA_KV_HEADS, NSA_GROUP, 3).transpose(0, 2, 3, 1, 4)
    o = _nsa_prompt(qm, ck_p, cv_p, sl_p(nkv).reshape(B, L, 4 * LANES),
                    sl_p(win).reshape(B, L, 2 * LANES), gt, ov_p, min(SEL_TOPN, n_sel_p))
    o = o.reshape(B, NSA_KV_HEADS, NSA_GROUP, L, NSA_KV_HEADS, HEAD_DIM)
    o = jnp.einsum("bkglkd->blkgd", o)
    o_nsa_p = o.reshape(Np, NSA_WIDTH)

    Lk = past + Lq
    n_cmp_s = (Lk - CMP_BLOCK) // CMP_STRIDE + 1
    n_sel_s = -(-Lk // SEL_BLOCK)
    ov_s = _overlap(ck_s.shape[1], _pad_to(n_sel_s, LANES), n_cmp_s, n_sel_s)
    qg = sl_s(nq).reshape(Bd, Lq, NSA_KV_HEADS, NSA_GROUP, HEAD_DIM).transpose(0, 3, 2, 1, 4)
    qm = _head_masked(qg, NSA_KV_HEADS).transpose(0, 2, 1, 3, 4).reshape(Bd, NSA_HEADS * Lq, LANES)
    gt = sl_s(gates)[:, :N_GATE_COLS].reshape(Bd, Lq, NSA_KV_HEADS, NSA_GROUP, 3)
    gt = gt.transpose(0, 2, 3, 1, 4).reshape(Bd, NSA_HEADS * Lq, 3)
    nkv_s = sl_s(nkv).reshape(Bd, Lq, 4 * LANES)
    wrows = win_all_s.reshape(Bd, -1, 2 * LANES)
    wpad = _pad_to(wrows.shape[1], PAGE_SIZE) - wrows.shape[1]
    wrows = jnp.pad(wrows, ((0, 0), (0, wpad), (0, 0)))
    key_blk = jnp.arange(past + PAGE_SIZE) // SEL_BLOCK
    ex = (jnp.arange(ov_s.shape[1])[:, None] == key_blk[None, :]).astype(BF16)
    o = _nsa_decode(page_table, qm, ck_s, cv_s,
                    pad_rows(nkv_s[:, :, 2 * LANES:3 * LANES]), pad_rows(nkv_s[:, :, 3 * LANES:]),
                    cache_nsa.reshape(-1, n_pool, PAGE_SIZE, 4 * LANES),
                    wrows[:, :, :LANES], wrows[:, :, LANES:], gt, ov_s, ex,
                    l, Lq, min(SEL_TOPN, n_sel_s))
    o = o.reshape(Bd, NSA_KV_HEADS, NSA_GROUP, Lq, NSA_KV_HEADS, HEAD_DIM)
    o = jnp.einsum("bkgqkd->bqkgd", o)
    o_nsa_s = o.reshape(Ns, NSA_WIDTH)

    tail = jnp.zeros((NT - Np - Ns, SB_WIDTH), F32)
    o_sb = jnp.concatenate([o_sb_p, o_sb_s, tail], axis=0)
    o_nsa = jnp.concatenate([o_nsa_p, o_nsa_s, tail], axis=0)
    x1 = _merge(xcat, o_sb, o_nsa, gab, W["w_br_a"][l], W["w_br_b"][l], W["w_out"][l],
                W["ln1_g"][l], W["ln1_b"][l])
    tt = 512 if NT % 512 == 0 else 256
    a_s, b, thr = _peer_route(x1, W["peer_wq"][l], W["peer_keys"][l], tt)
    x2 = _peer_dense(x1, a_s, b, thr, W["peer_u"][l], W["peer_vt"][l],
                     W["ln2_g"][l], W["ln2_b"][l], tt)

    keep_s = state_win.shape[2]
    keep_p = min(WINDOW, L)
    return x2, (new_sb_p, new_sb_s, new_nsa_p, new_nsa_s, win_p[:, L - keep_p:],
                win_all_s[:, win_all_s.shape[1] - keep_s:])


def _prep_weights(w_in, w_cmp1, w_cmp2, cmp_pe, w_br_a, w_br_b, w_out, ln1_g, ln1_b,
                  peer_wq, peer_keys, peer_u, peer_v, ln2_g, ln2_b):
    depth = w_in.shape[0]
    g0 = 2816
    w_in_p = jnp.concatenate(
        [w_in[:, :, :g0], w_in[:, :, g0 + N_GATE_COLS:], w_in[:, :, g0:g0 + N_GATE_COLS],
         jnp.zeros((depth, D_MODEL, LANES - N_GATE_COLS), w_in.dtype)], axis=2).astype(BF16)
    eye = jnp.eye(NSA_KV_HEADS, dtype=F32)
    cmp = []
    for l in range(depth):
        w1 = w_cmp1[l]
        half = lambda s: jnp.einsum("tjdh,gk->tjgdkh", w1[:, s], eye).reshape(
            2, CMP_STRIDE * LANES, 2 * LANES).astype(BF16)
        pe = cmp_pe[l]
        pe_half = lambda s: jnp.broadcast_to(
            pe[:, s, None, :], (2, CMP_STRIDE, NSA_KV_HEADS, HEAD_DIM)).reshape(2, 1, CMP_STRIDE * LANES)
        w2big = jnp.einsum("thd,gk->tghkd", w_cmp2[l], eye).reshape(2, 2 * LANES, LANES).astype(BF16)
        lo, hi = slice(0, CMP_STRIDE), slice(CMP_STRIDE, CMP_BLOCK)
        cmp.append((half(lo), half(hi), pe_half(lo), pe_half(hi), w2big))
    return {
        "w_in": w_in_p, "cmp": cmp,
        "w_br_a": w_br_a.astype(BF16), "w_br_b": w_br_b.astype(BF16), "w_out": w_out.astype(BF16),
        "ln1_g": ln1_g[:, None, :], "ln1_b": ln1_b[:, None, :],
        "ln2_g": ln2_g[:, None, :], "ln2_b": ln2_b[:, None, :],
        "peer_wq": peer_wq.astype(BF16), "peer_keys": peer_keys.astype(BF16),
        "peer_u": _expert_table_bf16(peer_u, False), "peer_vt": _expert_table_bf16(peer_v, True),
    }


def kernel(x_prompt, x_sample, cache_sb, cache_nsa, state_win, page_table, w_in, w_cmp1, w_cmp2,
           cmp_pe, w_br_a, w_br_b, w_out, ln1_g, ln1_b, peer_wq, peer_keys, peer_u, peer_v,
           ln2_g, ln2_b):
    B, L, D = x_prompt.shape
    Bd, Lq, _ = x_sample.shape
    n_pages = page_table.shape[1]
    past = n_pages * PAGE_SIZE
    Np, Ns = B * L, Bd * Lq
    NT = _pad_to(Np + Ns, 512)
    assert L % 256 == 0 and state_win.shape[2] == WINDOW and past >= WINDOW and Lq == SUBLANES

    W = _prep_weights(w_in, w_cmp1, w_cmp2, cmp_pe, w_br_a, w_br_b, w_out, ln1_g, ln1_b,
                      peer_wq, peer_keys, peer_u, peer_v, ln2_g, ln2_b)
    pos = jnp.concatenate([jnp.tile(jnp.arange(L), B), jnp.tile(past + jnp.arange(Lq), Bd),
                           jnp.zeros((NT - Np - Ns,), jnp.int32)])
    tables = _rope_tables(pos)
    xcat = jnp.concatenate([x_prompt.reshape(Np, D), x_sample.reshape(Ns, D),
                            jnp.zeros((NT - Np - Ns, D), x_prompt.dtype)], axis=0)
    geom = (B, L, Bd, Lq, n_pages)
    outs = []
    for l in range(w_in.shape[0]):
        xcat, leaves = _layer(l, xcat, geom, cache_sb, cache_nsa, state_win, page_table, tables, W)
        outs.append(leaves)
    stack = lambda k: jnp.stack([o[k] for o in outs])
    return (xcat[:Np].reshape(B, L, D), xcat[Np:Np + Ns].reshape(Bd, Lq, D),
            stack(0), stack(1), stack(2), stack(3), stack(4), stack(5))
```

```python
import functools
import math

import jax
import jax.numpy as jnp
from jax import lax
from jax.experimental import pallas as pl
from jax.experimental.pallas import tpu as pltpu

F32 = jnp.float32
BF16 = jnp.bfloat16

D_MODEL = 1024
PAGE_SIZE = 128
HEAD_DIM = 64
SB_HEADS = 8
NSA_HEADS = 8
NSA_KV_HEADS = 2
NSA_GROUP = NSA_HEADS // NSA_KV_HEADS
SB_WIDTH = SB_HEADS * HEAD_DIM
NSA_WIDTH = NSA_HEADS * HEAD_DIM
KV_WIDTH = NSA_KV_HEADS * HEAD_DIM
ROPE_DIM = HEAD_DIM // 4
ROPE_THETA = 500000.0
CMP_BLOCK = 32
CMP_STRIDE = 16
CMP_HIDDEN = 2 * HEAD_DIM
SEL_BLOCK = 64
SEL_TOPN = 16
SEL_FORCE = 1e4
WINDOW = 512
PEER_HEADS = 8
PEER_NKEYS = 128
PEER_DK = 256
PEER_TOPK = 16
DEPTH = 2
DN_ALPHA = (2 * DEPTH) ** 0.25
LN_EPS = 1e-5
NEG = -1e30
BELOW_NEG = -3e38
ATT_SCALE = HEAD_DIM ** -0.5
SB_UNDERFLOW = -104.0

LANES = 128
SUBLANES = 8
VMEM_LIMIT = 56 * 1024 * 1024

N_GATE_COLS = 3 * NSA_HEADS
PROJ_COLS = 2816 + 2 * D_MODEL + LANES


def _cparams(sem):
    return pltpu.CompilerParams(dimension_semantics=sem, vmem_limit_bytes=VMEM_LIMIT)


def _nt_dot(a, b):
    return lax.dot_general(a, b, (((1,), (1,)), ((), ())), preferred_element_type=F32)


def _dot(a, b):
    return jnp.dot(a, b, preferred_element_type=F32)


def _gelu(x):
    c = math.sqrt(2.0 / math.pi)
    return 0.5 * x * (1.0 + jnp.tanh(c * (x + 0.044715 * (x * x * x))))


def _sigmoid(x):
    return 1.0 / (1.0 + jnp.exp(-x))


def _softplus(z):
    return jnp.maximum(z, 0.0) + jnp.log1p(jnp.exp(-jnp.abs(z)))


def _layer_norm(y, g, b):
    mu = jnp.mean(y, axis=-1, keepdims=True)
    d = y - mu
    var = jnp.mean(d * d, axis=-1, keepdims=True)
    return d * lax.rsqrt(var + LN_EPS) * g + b


def _proj_kernel(x_ref, w_ref, cos_ref, sa_ref, sb_ref,
                 sbq_ref, sbkv_ref, nq_ref, nkv_ref, win_ref, gab_ref, gates_ref):
    x = x_ref[...].astype(BF16)
    cos, sa, sb = cos_ref[...], sa_ref[...], sb_ref[...]

    def mm(c0, c1):
        return _dot(x, w_ref[:, c0:c1])

    def rope(t):
        half = ROPE_DIM // 2
        return t * cos + pltpu.roll(t, half, 1) * sa + pltpu.roll(t, LANES - half, 1) * sb

    sbq_ref[...] = mm(0, 512)
    sbkv_ref[...] = mm(512, 1536)
    hq = mm(1536, 2048)
    for i in range(4):
        nq_ref[:, i * LANES:(i + 1) * LANES] = rope(hq[:, i * LANES:(i + 1) * LANES])
    hk = mm(2048, 2560)
    nkv_ref[:, 0:128] = rope(hk[:, 0:128])
    nkv_ref[:, 128:256] = hk[:, 128:256]
    nkv_ref[:, 256:384] = rope(hk[:, 256:384])
    nkv_ref[:, 384:512] = hk[:, 384:512]
    hw = mm(2560, 2816)
    win_ref[:, 0:128] = rope(hw[:, 0:128])
    win_ref[:, 128:256] = hw[:, 128:256]
    gab_ref[:, 0:1024] = _sigmoid(mm(2816, 3840))
    gab_ref[:, 1024:2048] = _sigmoid(mm(3840, 4864))
    gates_ref[...] = _sigmoid(mm(4864, 4992))


def _proj(x, w, cos, sa, sb, tm=256):
    nt = x.shape[0]
    row = lambda n: pl.BlockSpec((tm, n), lambda i: (i, 0))
    outs = [(512, F32), (1024, F32), (512, F32), (512, F32), (256, F32), (2048, F32), (128, F32)]
    return pl.pallas_call(
        _proj_kernel,
        grid=(nt // tm,),
        in_specs=[row(D_MODEL), pl.BlockSpec((D_MODEL, PROJ_COLS), lambda i: (0, 0)),
                  row(LANES), row(LANES), row(LANES)],
        out_specs=[row(n) for n, _ in outs],
        out_shape=[jax.ShapeDtypeStruct((nt, n), dt) for n, dt in outs],
        compiler_params=_cparams(("parallel",)),
        name="in_proj",
    )(x, w, cos, sa, sb)


def _sb_weights(z, tri, vis, carry):
    sp = _softplus(z)
    lk = -sp if vis is None else jnp.where(vis, -sp, 0.0)
    hi = lk.astype(BF16)
    lo = (lk - hi.astype(F32)).astype(BF16)
    between = _dot(hi, tri) + _dot(lo, tri) + carry
    w = jnp.exp(z - sp + between)
    if vis is not None:
        w = jnp.where(vis, w, 0.0)
    return w.astype(BF16), carry + jnp.sum(lk, axis=1, keepdims=True)


def _sb_prompt_kernel(q_ref, k_ref, v_ref, tri_ref, o_ref, *, tq):
    i = pl.program_id(2)
    qf = q_ref[...] * ATT_SCALE
    tri = tri_ref[...]
    n_h = LANES // HEAD_DIM
    lane = lax.broadcasted_iota(jnp.int32, (1, LANES), 1)
    in_head = [(lane >= h * HEAD_DIM) & (lane < (h + 1) * HEAD_DIM) for h in range(n_h)]
    qs = [jnp.where(m, qf, 0.0).astype(BF16) for m in in_head]
    qpos = i * tq + lax.broadcasted_iota(jnp.int32, (tq, 1), 0)
    kio = lax.broadcasted_iota(jnp.int32, (1, tq), 1)

    def cond(s):
        return (s[0] <= i) & (s[1] > SB_UNDERFLOW)

    def body(s):
        jj, _, accs, cars = s
        start = pl.multiple_of((i - jj) * tq, tq)
        k = k_ref[pl.ds(start, tq), :].astype(BF16)
        v = v_ref[pl.ds(start, tq), :].astype(BF16)
        vis = (start + kio) < qpos
        new_accs, new_cars = [], []
        for q, acc, car in zip(qs, accs, cars):
            w, car = _sb_weights(_nt_dot(q, k), tri, vis, car)
            new_accs.append(acc + _dot(w, v))
            new_cars.append(car)
        cmax = functools.reduce(jnp.maximum, [jnp.max(c) for c in new_cars])
        return jj + 1, cmax, tuple(new_accs), tuple(new_cars)

    init = (jnp.int32(0), jnp.float32(0.0),
            tuple(jnp.zeros((tq, LANES), F32) for _ in range(n_h)),
            tuple(jnp.zeros((tq, 1), F32) for _ in range(n_h)))
    accs = lax.while_loop(cond, body, init)[2]
    out = accs[0]
    for m, acc in zip(in_head[1:], accs[1:]):
        out = jnp.where(m, acc, out)
    o_ref[...] = out


def _strict_lower(n):
    r = lax.broadcasted_iota(jnp.int32, (n, n), 0)
    c = lax.broadcasted_iota(jnp.int32, (n, n), 1)
    return (r > c).astype(BF16)


def _sb_prompt(sbq, sbkv, B, L, tq=256):
    tq = min(tq, L)
    nq = L // tq
    pairs = SB_WIDTH // LANES
    blk = lambda b, h, i: (b * nq + i, h)
    return pl.pallas_call(
        functools.partial(_sb_prompt_kernel, tq=tq),
        grid=(B, pairs, nq),
        in_specs=[pl.BlockSpec((tq, LANES), blk),
                  pl.BlockSpec((L, LANES), lambda b, h, i: (b, h)),
                  pl.BlockSpec((L, LANES), lambda b, h, i: (b, pairs + h)),
                  pl.BlockSpec((tq, tq), lambda b, h, i: (0, 0))],
        out_specs=pl.BlockSpec((tq, LANES), blk),
        out_shape=jax.ShapeDtypeStruct((B * L, SB_WIDTH), F32),
        compiler_params=_cparams(("parallel", "parallel", "arbitrary")),
        name="sb_prompt",
    )(sbq, sbkv, sbkv, _strict_lower(tq))


def _sb_decode_kernel(pt_ref, q_ref, new_ref, cache_ref, tri_ref, o_ref, buf_ref, sem_ref,
                      *, layer, n_pages, lq):
    b = pl.program_id(0)
    rows = SB_HEADS * lq
    past = n_pages * PAGE_SIZE
    tri = tri_ref[...]
    qs = [(q_ref[0, h] * ATT_SCALE).astype(BF16) for h in range(SB_HEADS)]
    qpos = past + lax.broadcasted_iota(jnp.int32, (rows, 1), 0) % lq
    kio = lax.broadcasted_iota(jnp.int32, (1, PAGE_SIZE), 1)

    def page_copy(p, slot):
        return pltpu.make_async_copy(cache_ref.at[layer, pt_ref[b, p]], buf_ref.at[slot],
                                     sem_ref.at[slot])

    def block(kv, vis, car):
        z = jnp.concatenate([_nt_dot(qs[h], kv(0, h).astype(BF16)) for h in range(SB_HEADS)], axis=0)
        w, car = _sb_weights(z, tri, vis, car)
        contrib = jnp.concatenate(
            [_dot(w[h * lq:(h + 1) * lq], kv(1, h).astype(BF16)) for h in range(SB_HEADS)], axis=0)
        return contrib, car

    page_copy(n_pages - 1, 0).start()
    acc, car = block(lambda c, h: new_ref[0, :, c, h, :], (past + kio) < qpos,
                     jnp.zeros((rows, 1), F32))

    def cond(s):
        return (s[0] < n_pages) & (s[3] > SB_UNDERFLOW)

    def body(s):
        jj, acc, car, _ = s
        slot = jj % 2
        p = n_pages - 1 - jj
        page_copy(p, slot).wait()

        @pl.when(jj + 1 < n_pages)
        def _():
            page_copy(p - 1, 1 - slot).start()

        contrib, car = block(lambda c, h: buf_ref[slot, :, c, h, :], None, car)
        return jj + 1, acc + contrib, car, jnp.max(car)

    jj, acc, _, _ = lax.while_loop(cond, body, (jnp.int32(0), acc, car, jnp.max(car)))

    @pl.when(jj < n_pages)
    def _():
        page_copy(n_pages - 1 - jj, jj % 2).wait()

    o_ref[0] = acc


def _sb_decode(page_table, q, kv_new, cache, layer):
    bd, n_pages = page_table.shape
    lq = q.shape[2]
    rows = SB_HEADS * lq
    page_shape = (PAGE_SIZE, 2, SB_HEADS, HEAD_DIM)
    grid_spec = pltpu.PrefetchScalarGridSpec(
        num_scalar_prefetch=1,
        grid=(bd,),
        in_specs=[pl.BlockSpec((1, SB_HEADS, lq, HEAD_DIM), lambda b, pt: (b, 0, 0, 0)),
                  pl.BlockSpec((1,) + page_shape, lambda b, pt: (b, 0, 0, 0, 0)),
                  pl.BlockSpec(memory_space=pl.ANY),
                  pl.BlockSpec((PAGE_SIZE, PAGE_SIZE), lambda b, pt: (0, 0))],
        out_specs=pl.BlockSpec((1, rows, HEAD_DIM), lambda b, pt: (b, 0, 0)),
        scratch_shapes=[pltpu.VMEM((2,) + page_shape, F32), pltpu.SemaphoreType.DMA((2,))],
    )
    return pl.pallas_call(
        functools.partial(_sb_decode_kernel, layer=layer, n_pages=n_pages, lq=lq),
        grid_spec=grid_spec,
        out_shape=jax.ShapeDtypeStruct((bd, rows, HEAD_DIM), F32),
        compiler_params=_cparams(("arbitrary",)),
        name="sb_decode",
    )(page_table, q, kv_new, cache, _strict_lower(PAGE_SIZE))


def _cmp_partial_kernel(rk_ref, rv_ref, wa_ref, wb_ref, pea_ref, peb_ref, o_ref, xk_ref, xv_ref,
                        *, pg):
    nchunk = PAGE_SIZE // CMP_STRIDE

    def gather(p, c):
        dst = pl.ds(pl.multiple_of(p * nchunk, nchunk), nchunk)
        for j in range(CMP_STRIDE):
            src = pl.ds(j, nchunk, stride=CMP_STRIDE)
            xk_ref[dst, j * LANES:(j + 1) * LANES] = rk_ref[p, src, :]
            xv_ref[dst, j * LANES:(j + 1) * LANES] = rv_ref[p, src, :]
        return c

    lax.fori_loop(0, pg, gather, 0)
    for t, x_ref in enumerate((xk_ref, xv_ref)):
        x = x_ref[...]
        a = _dot((x + pea_ref[t]).astype(BF16), wa_ref[t])
        b = _dot((x + peb_ref[t]).astype(BF16), wb_ref[t])
        o_ref[:, t * 512:t * 512 + 256] = a
        o_ref[:, t * 512 + 256:(t + 1) * 512] = b


def _cmp_partial(rows, wa, wb, pea, peb, pg, n_pages, page0=0):
    nchunk = PAGE_SIZE // CMP_STRIDE
    kdim = CMP_STRIDE * LANES
    blk0 = page0 // pg
    const3 = lambda s: pl.BlockSpec(s, lambda i: (0, 0, 0))
    return pl.pallas_call(
        functools.partial(_cmp_partial_kernel, pg=pg),
        grid=(n_pages // pg,),
        in_specs=[pl.BlockSpec((pg, PAGE_SIZE, LANES), lambda i: (blk0 + i, 0, 0)),
                  pl.BlockSpec((pg, PAGE_SIZE, LANES), lambda i: (blk0 + i, 0, 1)),
                  const3((2, kdim, 2 * LANES)), const3((2, kdim, 2 * LANES)),
                  const3((2, 1, kdim)), const3((2, 1, kdim))],
        out_specs=pl.BlockSpec((pg * nchunk, 1024), lambda i: (i, 0)),
        out_shape=jax.ShapeDtypeStruct((n_pages * nchunk, 1024), F32),
        scratch_shapes=[pltpu.VMEM((pg * nchunk, kdim), F32), pltpu.VMEM((pg * nchunk, kdim), F32)],
        compiler_params=_cparams(("parallel",)),
        name="cmp_partial",
    )(rows, rows, wa, wb, pea, peb)


def _cmp_finish_kernel(pt_ref, ab_ref, w2_ref, ck_ref, cv_ref, buf_ref, sem_ref, *, pp):
    b = pl.program_id(0)
    nchunk = PAGE_SIZE // CMP_STRIDE

    def copy(j):
        return pltpu.make_async_copy(ab_ref.at[pt_ref[b, j]], buf_ref.at[j], sem_ref.at[j])

    for j in range(pp):
        copy(j).start()
    for j in range(pp):
        copy(j).wait()

    n = pp * nchunk
    x = buf_ref[...].reshape(n, 1024)
    last = lax.broadcasted_iota(jnp.int32, (n, 1), 0) == n - 1
    for t, o_ref in enumerate((ck_ref, cv_ref)):
        a = x[:, t * 512:t * 512 + 256]
        bnext = pltpu.roll(x[:, t * 512 + 256:(t + 1) * 512], n - 1, 0)
        hid = _gelu(a + jnp.where(last, 0.0, bnext))
        o_ref[0] = _dot(hid.astype(BF16), w2_ref[t])


def _cmp_finish(page_table, ab, w2big):
    nb, pp = page_table.shape
    nchunk = PAGE_SIZE // CMP_STRIDE
    n = pp * nchunk
    grid_spec = pltpu.PrefetchScalarGridSpec(
        num_scalar_prefetch=1,
        grid=(nb,),
        in_specs=[pl.BlockSpec(memory_space=pl.ANY),
                  pl.BlockSpec((2, 2 * LANES, LANES), lambda b, pt: (0, 0, 0))],
        out_specs=[pl.BlockSpec((1, n, LANES), lambda b, pt: (b, 0, 0))] * 2,
        scratch_shapes=[pltpu.VMEM((pp, nchunk, 1024), F32), pltpu.SemaphoreType.DMA((pp,))],
    )
    return pl.pallas_call(
        functools.partial(_cmp_finish_kernel, pp=pp),
        grid_spec=grid_spec,
        out_shape=[jax.ShapeDtypeStruct((nb, n, LANES), F32)] * 2,
        compiler_params=_cparams(("arbitrary",)),
        name="cmp_finish",
    )(page_table, ab, w2big)


def _topk_mask(score, n_top, axis):
    idx_f = lax.broadcasted_iota(jnp.int32, score.shape, axis).astype(F32)
    sel = jnp.zeros(score.shape, F32)
    work = score
    for _ in range(n_top):
        m = jnp.max(work, axis=axis, keepdims=True)
        first = jnp.min(jnp.where(work == m, idx_f, float(score.shape[axis])), axis=axis,
                        keepdims=True)
        hit = idx_f == first
        sel = jnp.where(hit & (m > 0.5 * NEG), 1.0, sel)
        work = jnp.where(hit, BELOW_NEG, work)
    return sel


def _cmp_branch(q, ck, cv, qpos):
    sc = _nt_dot(q, ck.astype(BF16))
    nio = lax.broadcasted_iota(jnp.int32, (1, sc.shape[1]), 1)
    cvis = (CMP_STRIDE * nio + (CMP_BLOCK - 1)) <= qpos
    scm = jnp.where(cvis, sc, NEG)
    e = jnp.exp(scm - jnp.max(scm, axis=-1, keepdims=True))
    pc = jnp.where(cvis, e / jnp.sum(e, axis=-1, keepdims=True), 0.0).astype(BF16)
    return _dot(pc, cv.astype(BF16)), pc


def _select_blocks(imp, qpos, n_top, axis):
    shape = (1, imp.shape[1]) if axis == 1 else (imp.shape[0], 1)
    sio = lax.broadcasted_iota(jnp.int32, shape, axis)
    cur = qpos // SEL_BLOCK
    forced = (sio == 0) | (sio == cur) | (sio == cur - 1)
    imp = jnp.where(sio <= cur, imp + jnp.where(forced, SEL_FORCE, 0.0), NEG)
    return _topk_mask(imp, n_top, axis)


def _softmax_step(state, s, mask, v, every_row_seen=False):
    m, l, acc = state
    sm = jnp.where(mask, s, NEG)
    m_new = jnp.maximum(m, jnp.max(sm, axis=-1, keepdims=True))
    alpha = jnp.exp(m - m_new)
    p = jnp.exp(sm - m_new)
    if not every_row_seen:
        p = jnp.where(mask, p, 0.0)
    return (m_new, alpha * l + jnp.sum(p, axis=-1, keepdims=True),
            alpha * acc + _dot(p.astype(BF16), v))


def _softmax_init(r):
    return (jnp.full((r, 1), NEG, F32), jnp.zeros((r, 1), F32), jnp.zeros((r, LANES), F32))


def _nsa_prompt_kernel(q_ref, ck_ref, cv_ref, ks_ref, vs_ref, kw_ref, vw_ref, g_ref, ovt_ref,
                       o_ref, *, tq, n_top):
    i = pl.program_id(2)
    r = NSA_GROUP * tq
    q = (q_ref[0, 0].reshape(r, LANES) * ATT_SCALE).astype(BF16)
    qpos = i * tq + lax.broadcasted_iota(jnp.int32, (r, 1), 0) % tq

    o_c, pc = _cmp_branch(q, ck_ref[0], cv_ref[0], qpos)
    ovt = ovt_ref[...]
    imp_t = _nt_dot(ovt, pc[0:tq])
    for g in range(1, NSA_GROUP):
        imp_t = imp_t + _nt_dot(ovt, pc[g * tq:(g + 1) * tq])
    qpos_t = i * tq + lax.broadcasted_iota(jnp.int32, (1, tq), 1)
    sel = _select_blocks(imp_t, qpos_t, n_top, 0).T.astype(BF16)
    ns = sel.shape[1]

    tk = 2 * tq
    per = tk // SEL_BLOCK
    sio = lax.broadcasted_iota(jnp.int32, (ns, tk), 0)
    kio_s = lax.broadcasted_iota(jnp.int32, (ns, tk), 1) // SEL_BLOCK
    kio = lax.broadcasted_iota(jnp.int32, (1, tk), 1)

    def sel_body(j, state):
        start = pl.multiple_of(j * tk, tk)
        k = ks_ref[0, pl.ds(start, tk), :].astype(BF16)
        v = vs_ref[0, pl.ds(start, tk), :].astype(BF16)
        expand = (sio == kio_s + j * per).astype(BF16)
        picked = _dot(sel, expand)
        picked = jnp.concatenate([picked] * NSA_GROUP, axis=0)
        mask = (picked > 0.5) & ((start + kio) <= qpos)
        return _softmax_step(state, _nt_dot(q, k), mask, v, every_row_seen=True)

    n_sel_steps = (i * tq + tq - 1) // tk + 1
    m_s, l_s, acc_s = lax.fori_loop(0, n_sel_steps, sel_body, _softmax_init(r))

    wio = lax.broadcasted_iota(jnp.int32, (1, tq), 1)

    def win_body(jj, state):
        start = pl.multiple_of((i - jj) * tq, tq)
        k = kw_ref[0, pl.ds(start, tq), :].astype(BF16)
        v = vw_ref[0, pl.ds(start, tq), :].astype(BF16)
        rel = qpos - (start + wio)
        mask = (rel >= 0) & (rel < WINDOW)
        return _softmax_step(state, _nt_dot(q, k), mask, v, every_row_seen=True)

    m_w, l_w, acc_w = lax.fori_loop(0, jnp.minimum(i, WINDOW // tq) + 1, win_body,
                                    _softmax_init(r))

    g = g_ref[0, 0].reshape(r, 3)
    o = o_c * g[:, 0:1] + (acc_s / l_s) * g[:, 1:2] + (acc_w / l_w) * g[:, 2:3]
    o_ref[0, 0] = o.reshape(NSA_GROUP, tq, LANES)


def _nsa_prompt(qm, ck, cv, nkv, win, gates, ov, n_top, tq=128):
    B, _, _, L, _ = qm.shape
    nc = ck.shape[1]
    seq = lambda lane_blk: pl.BlockSpec((1, L, LANES), lambda b, h, i: (b, 0, lane_blk))
    qspec = lambda last: pl.BlockSpec((1, 1, NSA_GROUP, tq, last), lambda b, h, i: (b, h, 0, i, 0))
    cspec = pl.BlockSpec((1, nc, LANES), lambda b, h, i: (b, 0, 0))
    return pl.pallas_call(
        functools.partial(_nsa_prompt_kernel, tq=tq, n_top=n_top),
        grid=(B, NSA_KV_HEADS, L // tq),
        in_specs=[qspec(LANES), cspec, cspec, seq(2), seq(3), seq(0), seq(1), qspec(3),
                  pl.BlockSpec(ov.shape, lambda b, h, i: (0, 0))],
        out_specs=qspec(LANES),
        out_shape=jax.ShapeDtypeStruct(qm.shape, F32),
        compiler_params=_cparams(("parallel", "parallel", "arbitrary")),
        name="nsa_prompt",
    )(qm, ck, cv, nkv, nkv, win, win, gates, ov)


def _nsa_decode_kernel(pt_ref, q_ref, ck_ref, cv_ref, kn_ref, vn_ref, cache_ref, kw_ref, vw_ref,
                       g_ref, ov_ref, ex_ref, o_ref, buf_ref, sem_ref,
                       *, page0, n_pages, lq, n_top):
    b = pl.program_id(0)
    r = NSA_HEADS * lq
    rk = NSA_KV_HEADS * lq
    past = n_pages * PAGE_SIZE

    def page_copy(j):
        return pltpu.make_async_copy(
            cache_ref.at[page0 + pt_ref[b, j], :, pl.ds(2 * LANES, 2 * LANES)],
            buf_ref.at[j], sem_ref.at[j])

    for j in range(n_pages):
        page_copy(j).start()

    q = (q_ref[0] * ATT_SCALE).astype(BF16)
    qpos = past + lax.broadcasted_iota(jnp.int32, (r, 1), 0) % lq
    kio = lax.broadcasted_iota(jnp.int32, (1, PAGE_SIZE), 1)

    o_c, pc = _cmp_branch(q, ck_ref[0], cv_ref[0], qpos)
    impg = _dot(pc, ov_ref[...])
    ns = impg.shape[1]
    impg = impg.reshape(NSA_KV_HEADS, NSA_GROUP, lq, ns)
    imp = impg[:, 0]
    for g in range(1, NSA_GROUP):
        imp = imp + impg[:, g]
    qpos1 = past + lax.broadcasted_iota(jnp.int32, (rk, 1), 0) % lq
    sel = _select_blocks(imp.reshape(rk, ns), qpos1, n_top, 1)
    nk = past + PAGE_SIZE
    picked = _dot(sel.astype(BF16), ex_ref[...])
    picked = jnp.broadcast_to(picked.reshape(NSA_KV_HEADS, 1, lq, nk),
                              (NSA_KV_HEADS, NSA_GROUP, lq, nk)).reshape(r, nk)
    kpos = lax.broadcasted_iota(jnp.int32, (1, nk), 1)
    mask = (picked > 0.5) & (kpos <= qpos)

    for j in range(n_pages):
        page_copy(j).wait()
    k_past = buf_ref[:, :, 0:LANES].reshape(past, LANES).astype(BF16)
    v_past = buf_ref[:, :, LANES:2 * LANES].reshape(past, LANES).astype(BF16)
    s = jnp.concatenate([_nt_dot(q, k_past), _nt_dot(q, kn_ref[0].astype(BF16))], axis=1)
    sm = jnp.where(mask, s, NEG)
    p = jnp.where(mask, jnp.exp(sm - jnp.max(sm, axis=-1, keepdims=True)), 0.0)
    l_s = jnp.sum(p, axis=-1, keepdims=True)
    p = p.astype(BF16)
    acc_s = _dot(p[:, :past], v_past) + _dot(p[:, past:], vn_ref[0].astype(BF16))

    state = _softmax_init(r)
    for w in range(kw_ref.shape[1] // PAGE_SIZE):
        k = kw_ref[0, w * PAGE_SIZE:(w + 1) * PAGE_SIZE, :].astype(BF16)
        v = vw_ref[0, w * PAGE_SIZE:(w + 1) * PAGE_SIZE, :].astype(BF16)
        rel = qpos - (past - WINDOW + w * PAGE_SIZE + kio)
        state = _softmax_step(state, _nt_dot(q, k), (rel >= 0) & (rel < WINDOW), v)
    _, l_w, acc_w = state
    g = g_ref[0]
    o_ref[0] = o_c * g[:, 0:1] + (acc_s / l_s) * g[:, 1:2] + (acc_w / l_w) * g[:, 2:3]


def _nsa_decode(page_table, qm, ck, cv, k_new, v_new, cache, kwin, vwin, gates, ov, ex,
                page0, lq, n_top):
    bd, n_pages = page_table.shape
    r = NSA_HEADS * lq
    per_b = lambda a: pl.BlockSpec((1,) + a.shape[1:], lambda b, pt: (b, 0, 0))
    const = lambda a: pl.BlockSpec(a.shape, lambda b, pt: (0, 0))
    grid_spec = pltpu.PrefetchScalarGridSpec(
        num_scalar_prefetch=1,
        grid=(bd,),
        in_specs=[per_b(qm), per_b(ck), per_b(cv), per_b(k_new), per_b(v_new),
                  pl.BlockSpec(memory_space=pl.ANY),
                  per_b(kwin), per_b(vwin), per_b(gates), const(ov), const(ex)],
        out_specs=pl.BlockSpec((1, r, LANES), lambda b, pt: (b, 0, 0)),
        scratch_shapes=[pltpu.VMEM((n_pages, PAGE_SIZE, 2 * LANES), F32),
                        pltpu.SemaphoreType.DMA((n_pages,))],
    )
    return pl.pallas_call(
        functools.partial(_nsa_decode_kernel, page0=page0, n_pages=n_pages, lq=lq, n_top=n_top),
        grid_spec=grid_spec,
        out_shape=jax.ShapeDtypeStruct((bd, r, LANES), F32),
        compiler_params=_cparams(("arbitrary",)),
        name="nsa_decode",
    )(page_table, qm, ck, cv, k_new, v_new, cache, kwin, vwin, gates, ov, ex)


def _merge_kernel(x_ref, oa_ref, ob_ref, gab_ref, wa_ref, wb_ref, wo_ref, g_ref, b_ref, y_ref):
    a = _dot(oa_ref[...].astype(BF16), wa_ref[...])
    b = _dot(ob_ref[...].astype(BF16), wb_ref[...])
    merged = gab_ref[:, 0:D_MODEL] * a + gab_ref[:, D_MODEL:2 * D_MODEL] * b
    y = DN_ALPHA * x_ref[...] + _dot(merged.astype(BF16), wo_ref[...])
    y_ref[...] = _layer_norm(y, g_ref[...], b_ref[...])


def _merge(x, oa, ob, gab, wa, wb, wo, g, b, tm=256):
    nt = x.shape[0]
    row = lambda n: pl.BlockSpec((tm, n), lambda i: (i, 0))
    const = lambda a: pl.BlockSpec(a.shape, lambda i: (0, 0))
    return pl.pallas_call(
        _merge_kernel,
        grid=(nt // tm,),
        in_specs=[row(D_MODEL), row(SB_WIDTH), row(NSA_WIDTH), row(2 * D_MODEL),
                  const(wa), const(wb), const(wo), const(g), const(b)],
        out_specs=row(D_MODEL),
        out_shape=jax.ShapeDtypeStruct((nt, D_MODEL), F32),
        compiler_params=_cparams(("parallel",)),
        name="merge_ln1",
    )(x, oa, ob, gab, wa, wb, wo, g, b)


def _top_values(x, k):
    n = x.shape[0]
    rio = lax.broadcasted_iota(jnp.int32, x.shape, 0).astype(F32)
    vals = []
    for _ in range(k):
        m = jnp.max(x, axis=0, keepdims=True)
        idx = jnp.min(jnp.where(x == m, rio, float(n)), axis=0, keepdims=True)
        x = jnp.where(rio == idx, BELOW_NEG, x)
        vals.append(m)
    return vals


def _pair_products(a, b):
    rows = [a[i] * b[j] for i in range(len(a)) for j in range(len(b))
            if (i + 1) * (j + 1) <= PEER_TOPK]
    pad = (-len(rows)) % SUBLANES
    rows += [jnp.full_like(rows[0], BELOW_NEG)] * pad
    return jnp.concatenate(rows, axis=0)


def _peer_route_kernel(x_ref, wq_ref, keys_ref, as_ref, b_ref, thr_ref):
    x = x_ref[...].astype(BF16)
    half = PEER_DK // 2
    for h in range(PEER_HEADS):
        qh = _dot(x, wq_ref[:, h * PEER_DK:(h + 1) * PEER_DK]).astype(BF16)
        s1 = _nt_dot(keys_ref[h, 0], qh[:, :half])
        s2 = _nt_dot(keys_ref[h, 1], qh[:, half:])
        t1 = _top_values(s1, PEER_TOPK)
        t2 = _top_values(s2, PEER_TOPK)
        a_un = jnp.exp(s1 - t1[0])
        b = jnp.exp(s2 - t2[0])
        a_top = [jnp.exp(v - t1[0]) for v in t1]
        b_top = [jnp.exp(v - t2[0]) for v in t2]
        z = sum(_top_values(_pair_products(a_top, b_top), PEER_TOPK))
        zinv = 1.0 / z
        thr = _top_values(_pair_products([v * zinv for v in a_top], b_top), PEER_TOPK)[-1]
        as_ref[h] = a_un * zinv
        b_ref[h] = b
        thr_ref[h] = thr


def _peer_route(x, wq, keys, tt):
    nt = x.shape[0]
    hspec = lambda n: pl.BlockSpec((PEER_HEADS, n, tt), lambda i: (0, 0, i))
    return pl.pallas_call(
        _peer_route_kernel,
        grid=(nt // tt,),
        in_specs=[pl.BlockSpec((tt, D_MODEL), lambda i: (i, 0)),
                  pl.BlockSpec(wq.shape, lambda i: (0, 0)),
                  pl.BlockSpec(keys.shape, lambda i: (0, 0, 0, 0))],
        out_specs=[hspec(PEER_NKEYS), hspec(PEER_NKEYS), hspec(1)],
        out_shape=[jax.ShapeDtypeStruct((PEER_HEADS, PEER_NKEYS, nt), F32),
                   jax.ShapeDtypeStruct((PEER_HEADS, PEER_NKEYS, nt), F32),
                   jax.ShapeDtypeStruct((PEER_HEADS, 1, nt), F32)],
        compiler_params=_cparams(("parallel",)),
        name="peer_route",
    )(x, wq, keys)


def _peer_dense_kernel(x_ref, as_ref, b_ref, thr_ref, u_ref, vt_ref, g_ref, bb_ref, y_ref,
                       acc_ref, act_ref, gate_ref, *, rows_per_step, n_split):
    c = pl.program_id(1)

    @pl.when(c == 0)
    def _():
        acc_ref[...] = jnp.zeros_like(acc_ref)

    xb = x_ref[...].astype(BF16)
    tt = xb.shape[0]
    rows_g = rows_per_step // n_split
    ne_g = rows_g * PEER_NKEYS
    def first_matmul(s):
        act_ref[s % 2] = _nt_dot(u_ref[s * ne_g:(s + 1) * ne_g, :], xb)

    first_matmul(0)
    for s in range(n_split):
        slot = s % 2
        if s + 1 < n_split:
            first_matmul(s + 1)
        for r in range(rows_g):
            row = s * rows_g + r
            base = pl.multiple_of(c * rows_per_step + row // SUBLANES * SUBLANES, SUBLANES)
            blk = slice(r * PEER_NKEYS, (r + 1) * PEER_NKEYS)
            for lg in range(tt // LANES):
                ls = slice(lg * LANES, (lg + 1) * LANES)
                w = None
                for h in range(PEER_HEADS):
                    a8 = as_ref[h, pl.ds(base, SUBLANES), ls]
                    p = b_ref[h, :, ls] * a8[row % SUBLANES:row % SUBLANES + 1, :]
                    wh = jnp.where(p >= thr_ref[h, :, ls], p, 0.0)
                    w = wh if w is None else w + wh
                gate_ref[slot, blk, ls] = (w * _gelu(act_ref[slot, blk, ls])).astype(BF16)
        acc_ref[...] += _dot(vt_ref[:, s * ne_g:(s + 1) * ne_g], gate_ref[slot])

    @pl.when(c == pl.num_programs(1) - 1)
    def _():
        y = DN_ALPHA * x_ref[...] + acc_ref[...].T
        y_ref[...] = _layer_norm(y, g_ref[...], bb_ref[...])


def _peer_dense(x, a_s, b, thr, u, vt, g, bb, tt, rows_per_step=16, n_split=4):
    nt = x.shape[0]
    ne = rows_per_step * PEER_NKEYS
    n_exp = u.shape[0]
    hspec = lambda n: pl.BlockSpec((PEER_HEADS, n, tt), lambda i, c: (0, 0, i))
    const = lambda a: pl.BlockSpec(a.shape, lambda i, c: (0, 0))
    return pl.pallas_call(
        functools.partial(_peer_dense_kernel, rows_per_step=rows_per_step, n_split=n_split),
        grid=(nt // tt, n_exp // ne),
        in_specs=[pl.BlockSpec((tt, D_MODEL), lambda i, c: (i, 0)),
                  hspec(PEER_NKEYS), hspec(PEER_NKEYS), hspec(1),
                  pl.BlockSpec((ne, D_MODEL), lambda i, c: (c, 0)),
                  pl.BlockSpec((D_MODEL, ne), lambda i, c: (0, c)),
                  const(g), const(bb)],
        out_specs=pl.BlockSpec((tt, D_MODEL), lambda i, c: (i, 0)),
        out_shape=jax.ShapeDtypeStruct((nt, D_MODEL), F32),
        scratch_shapes=[pltpu.VMEM((D_MODEL, tt), F32),
                        pltpu.VMEM((2, ne // n_split, tt), F32),
                        pltpu.VMEM((2, ne // n_split, tt), BF16)],
        compiler_params=_cparams(("parallel", "arbitrary")),
        name="peer_dense",
    )(x, a_s, b, thr, u, vt, g, bb)


def _cast_kernel(x_ref, o_ref, *, transpose):
    x = x_ref[0]
    o_ref[0] = (x.T if transpose else x).astype(BF16)


def _expert_table_bf16(w, transpose, tr=512):
    depth, n_exp, d = w.shape
    if transpose:
        out_shape, out_spec = (depth, d, n_exp), pl.BlockSpec((1, d, tr), lambda l, i: (l, 0, i))
    else:
        out_shape, out_spec = (depth, n_exp, d), pl.BlockSpec((1, tr, d), lambda l, i: (l, i, 0))
    return pl.pallas_call(
        functools.partial(_cast_kernel, transpose=transpose),
        grid=(depth, n_exp // tr),
        in_specs=[pl.BlockSpec((1, tr, d), lambda l, i: (l, i, 0))],
        out_specs=out_spec,
        out_shape=jax.ShapeDtypeStruct(out_shape, BF16),
        compiler_params=_cparams(("parallel", "parallel")),
        name="expert_table_bf16",
    )(w)


def _rope_tables(pos):
    half = ROPE_DIM // 2
    inv_freq = ROPE_THETA ** (-jnp.arange(half, dtype=F32) * (2.0 / ROPE_DIM))
    ang = pos.astype(F32)[:, None] * inv_freq[None, :]
    cos, sin = jnp.cos(ang), jnp.sin(ang)
    n = pos.shape[0]
    ones = jnp.ones((n, HEAD_DIM - ROPE_DIM), F32)
    zeros = jnp.zeros((n, HEAD_DIM - ROPE_DIM), F32)
    zh = jnp.zeros((n, half), F32)
    c = jnp.concatenate([cos, cos, ones], axis=1)
    sa = jnp.concatenate([zh, sin, zeros], axis=1)
    sb = jnp.concatenate([-sin, zh, zeros], axis=1)
    two = lambda t: jnp.concatenate([t, t], axis=1)
    return two(c), two(sa), two(sb)


def _overlap(n_cmp_pad, n_sel_pad, n_cmp, n_sel):
    c0 = (CMP_STRIDE * jnp.arange(n_cmp_pad))[:, None]
    s0 = (SEL_BLOCK * jnp.arange(n_sel_pad))[None, :]
    ov = jnp.clip(jnp.minimum(c0 + CMP_BLOCK, s0 + SEL_BLOCK) - jnp.maximum(c0, s0), 0)
    ov = ov.astype(F32) * (1.0 / CMP_BLOCK)
    valid = (jnp.arange(n_cmp_pad) < n_cmp)[:, None] & (jnp.arange(n_sel_pad) < n_sel)[None, :]
    return jnp.where(valid, ov, 0.0).astype(BF16)


def _pad_to(n, m):
    return -(-n // m) * m


def _largest_divisor(n, cap):
    return max(d for d in range(1, cap + 1) if n % d == 0)


def _head_masked(t, n_heads):
    eye = jnp.eye(n_heads, dtype=t.dtype)
    out = t[..., :, :, None, :] * eye[:, None, :, None]
    return out.reshape(t.shape[:-1] + (n_heads * HEAD_DIM,))


def _layer(l, xcat, geom, cache_sb, cache_nsa, state_win, page_table, tables, W):
    B, L, Bd, Lq, n_pages = geom
    past = n_pages * PAGE_SIZE
    Np, Ns = B * L, Bd * Lq
    NT = xcat.shape[0]
    cos, sa, sb = tables

    sbq, sbkv, nq, nkv, win, gab, gates = _proj(xcat, W["w_in"][l], cos, sa, sb)
    sl_p = lambda a: a[:Np]
    sl_s = lambda a: a[Np:Np + Ns]

    new_sb_p = sl_p(sbkv).reshape(B, L, 2, SB_HEADS, HEAD_DIM)
    new_sb_s = sl_s(sbkv).reshape(Bd, Lq, 2, SB_HEADS, HEAD_DIM)
    new_nsa_p = sl_p(nkv).reshape(B, L, 4, NSA_KV_HEADS, HEAD_DIM)
    new_nsa_s = sl_s(nkv).reshape(Bd, Lq, 4, NSA_KV_HEADS, HEAD_DIM)
    win_p = sl_p(win).reshape(B, L, 2, NSA_KV_HEADS, HEAD_DIM)
    win_all_s = jnp.concatenate(
        [state_win[l], sl_s(win).reshape(Bd, Lq, 2, NSA_KV_HEADS, HEAD_DIM)], axis=1)

    o_sb_p = _sb_prompt(sbq, sbkv, B, L)

    n_pool = cache_sb.shape[1]
    qd = sl_s(sbq).reshape(Bd, Lq, SB_HEADS, HEAD_DIM).transpose(0, 2, 1, 3)
    pad_rows = lambda t: jnp.pad(t, ((0, 0), (0, PAGE_SIZE - Lq)) + ((0, 0),) * (t.ndim - 2))
    o = _sb_decode(page_table, qd, pad_rows(new_sb_s), cache_sb, l)
    o_sb_s = o.reshape(Bd, SB_HEADS, Lq, HEAD_DIM).transpose(0, 2, 1, 3).reshape(Ns, SB_WIDTH)

    cw = W["cmp"][l]
    ab = _cmp_partial(nkv.reshape(NT // PAGE_SIZE, PAGE_SIZE, 4 * LANES), *cw[:4],
                      pg=_largest_divisor(NT // PAGE_SIZE, 33), n_pages=NT // PAGE_SIZE)
    ident = jnp.arange(Np // PAGE_SIZE, dtype=jnp.int32).reshape(B, L // PAGE_SIZE)
    ck_p, cv_p = _cmp_finish(ident, ab.reshape(NT // PAGE_SIZE, -1, 1024), cw[4])
    nsa_pages = cache_nsa.reshape(-1, PAGE_SIZE, 4 * LANES)
    ab = _cmp_partial(nsa_pages, *cw[:4],
                      pg=_largest_divisor(n_pool, 32), n_pages=n_pool, page0=l * n_pool)
    ck_s, cv_s = _cmp_finish(page_table, ab.reshape(n_pool, -1, 1024), cw[4])

    n_cmp_p = (L - CMP_BLOCK) // CMP_STRIDE + 1
    n_sel_p = -(-L // SEL_BLOCK)
    ov_p = _overlap(ck_p.shape[1], _pad_to(n_sel_p, LANES), n_cmp_p, n_sel_p).T
    qg = sl_p(nq).reshape(B, L, NSA_KV_HEADS, NSA_GROUP, HEAD_DIM).transpose(0, 2, 3, 1, 4)
    qm = _head_masked(qg.transpose(0, 2, 1, 3, 4), NSA_KV_HEADS).transpose(0, 2, 1, 3, 4)
    gt = sl_p(gates)[:, :N_GATE_COLS].reshape(B, L, NSA_KV_HEADS, NSA_GROUP, 3).transpose(0, 2, 3, 1, 4)
    o = _nsa_prompt(qm, ck_p, cv_p, sl_p(nkv).reshape(B, L, 4 * LANES),
                    sl_p(win).reshape(B, L, 2 * LANES), gt, ov_p, min(SEL_TOPN, n_sel_p))
    o = o.reshape(B, NSA_KV_HEADS, NSA_GROUP, L, NSA_KV_HEADS, HEAD_DIM)
    o = jnp.einsum("bkglkd->blkgd", o)
    o_nsa_p = o.reshape(Np, NSA_WIDTH)

    Lk = past + Lq
    n_cmp_s = (Lk - CMP_BLOCK) // CMP_STRIDE + 1
    n_sel_s = -(-Lk // SEL_BLOCK)
    ov_s = _overlap(ck_s.shape[1], _pad_to(n_sel_s, LANES), n_cmp_s, n_sel_s)
    qg = sl_s(nq).reshape(Bd, Lq, NSA_KV_HEADS, NSA_GROUP, HEAD_DIM).transpose(0, 3, 2, 1, 4)
    qm = _head_masked(qg, NSA_KV_HEADS).transpose(0, 2, 1, 3, 4).reshape(Bd, NSA_HEADS * Lq, LANES)
    gt = sl_s(gates)[:, :N_GATE_COLS].reshape(Bd, Lq, NSA_KV_HEADS, NSA_GROUP, 3)
    gt = gt.transpose(0, 2, 3, 1, 4).reshape(Bd, NSA_HEADS * Lq, 3)
    nkv_s = sl_s(nkv).reshape(Bd, Lq, 4 * LANES)
    wrows = win_all_s.reshape(Bd, -1, 2 * LANES)
    wpad = _pad_to(wrows.shape[1], PAGE_SIZE) - wrows.shape[1]
    wrows = jnp.pad(wrows, ((0, 0), (0, wpad), (0, 0)))
    key_blk = jnp.arange(past + PAGE_SIZE) // SEL_BLOCK
    ex = (jnp.arange(ov_s.shape[1])[:, None] == key_blk[None, :]).astype(BF16)
    o = _nsa_decode(page_table, qm, ck_s, cv_s,
                    pad_rows(nkv_s[:, :, 2 * LANES:3 * LANES]), pad_rows(nkv_s[:, :, 3 * LANES:]),
                    nsa_pages, wrows[:, :, :LANES], wrows[:, :, LANES:], gt, ov_s, ex,
                    l * n_pool, Lq, min(SEL_TOPN, n_sel_s))
    o = o.reshape(Bd, NSA_KV_HEADS, NSA_GROUP, Lq, NSA_KV_HEADS, HEAD_DIM)
    o = jnp.einsum("bkgqkd->bqkgd", o)
    o_nsa_s = o.reshape(Ns, NSA_WIDTH)

    tail = jnp.zeros((NT - Np - Ns, SB_WIDTH), F32)
    o_sb = jnp.concatenate([o_sb_p, o_sb_s, tail], axis=0)
    o_nsa = jnp.concatenate([o_nsa_p, o_nsa_s, tail], axis=0)
    x1 = _merge(xcat, o_sb, o_nsa, gab, W["w_br_a"][l], W["w_br_b"][l], W["w_out"][l],
                W["ln1_g"][l], W["ln1_b"][l])
    tt = 512 if NT % 512 == 0 else 256
    a_s, b, thr = _peer_route(x1, W["peer_wq"][l], W["peer_keys"][l], tt)
    x2 = _peer_dense(x1, a_s, b, thr, W["peer_u"][l], W["peer_vt"][l],
                     W["ln2_g"][l], W["ln2_b"][l], tt)

    keep_s = state_win.shape[2]
    keep_p = min(WINDOW, L)
    return x2, (new_sb_p, new_sb_s, new_nsa_p, new_nsa_s, win_p[:, L - keep_p:],
                win_all_s[:, win_all_s.shape[1] - keep_s:])


def _prep_weights(w_in, w_cmp1, w_cmp2, cmp_pe, w_br_a, w_br_b, w_out, ln1_g, ln1_b,
                  peer_wq, peer_keys, peer_u, peer_v, ln2_g, ln2_b):
    depth = w_in.shape[0]
    g0 = 2816
    w_in_p = jnp.concatenate(
        [w_in[:, :, :g0], w_in[:, :, g0 + N_GATE_COLS:], w_in[:, :, g0:g0 + N_GATE_COLS],
         jnp.zeros((depth, D_MODEL, LANES - N_GATE_COLS), w_in.dtype)], axis=2).astype(BF16)
    eye = jnp.eye(NSA_KV_HEADS, dtype=F32)
    cmp = []
    for l in range(depth):
        w1 = w_cmp1[l]
        half = lambda s: jnp.einsum("tjdh,gk->tjgdkh", w1[:, s], eye).reshape(
            2, CMP_STRIDE * LANES, 2 * LANES).astype(BF16)
        pe = cmp_pe[l]
        pe_half = lambda s: jnp.broadcast_to(
            pe[:, s, None, :], (2, CMP_STRIDE, NSA_KV_HEADS, HEAD_DIM)).reshape(2, 1, CMP_STRIDE * LANES)
        w2big = jnp.einsum("thd,gk->tghkd", w_cmp2[l], eye).reshape(2, 2 * LANES, LANES).astype(BF16)
        lo, hi = slice(0, CMP_STRIDE), slice(CMP_STRIDE, CMP_BLOCK)
        cmp.append((half(lo), half(hi), pe_half(lo), pe_half(hi), w2big))
    return {
        "w_in": w_in_p, "cmp": cmp,
        "w_br_a": w_br_a.astype(BF16), "w_br_b": w_br_b.astype(BF16), "w_out": w_out.astype(BF16),
        "ln1_g": ln1_g[:, None, :], "ln1_b": ln1_b[:, None, :],
        "ln2_g": ln2_g[:, None, :], "ln2_b": ln2_b[:, None, :],
        "peer_wq": peer_wq.astype(BF16), "peer_keys": peer_keys.astype(BF16),
        "peer_u": _expert_table_bf16(peer_u, False), "peer_vt": _expert_table_bf16(peer_v, True),
    }


def kernel(x_prompt, x_sample, cache_sb, cache_nsa, state_win, page_table, w_in, w_cmp1, w_cmp2,
           cmp_pe, w_br_a, w_br_b, w_out, ln1_g, ln1_b, peer_wq, peer_keys, peer_u, peer_v,
           ln2_g, ln2_b):
    B, L, D = x_prompt.shape
    Bd, Lq, _ = x_sample.shape
    n_pages = page_table.shape[1]
    past = n_pages * PAGE_SIZE
    Np, Ns = B * L, Bd * Lq
    NT = _pad_to(Np + Ns, 512)
    assert L % 256 == 0 and state_win.shape[2] == WINDOW and past >= WINDOW and Lq == SUBLANES

    W = _prep_weights(w_in, w_cmp1, w_cmp2, cmp_pe, w_br_a, w_br_b, w_out, ln1_g, ln1_b,
                      peer_wq, peer_keys, peer_u, peer_v, ln2_g, ln2_b)
    pos = jnp.concatenate([jnp.tile(jnp.arange(L), B), jnp.tile(past + jnp.arange(Lq), Bd),
                           jnp.zeros((NT - Np - Ns,), jnp.int32)])
    tables = _rope_tables(pos)
    xcat = jnp.concatenate([x_prompt.reshape(Np, D), x_sample.reshape(Ns, D),
                            jnp.zeros((NT - Np - Ns, D), x_prompt.dtype)], axis=0)
    geom = (B, L, Bd, Lq, n_pages)
    outs = []
    for l in range(w_in.shape[0]):
        xcat, leaves = _layer(l, xcat, geom, cache_sb, cache_nsa, state_win, page_table, tables, W)
        outs.append(leaves)
    stack = lambda k: jnp.stack([o[k] for o in outs])
    return (xcat[:Np].reshape(B, L, D), xcat[Np:Np + Ns].reshape(Bd, Lq, D),
            stack(0), stack(1), stack(2), stack(3), stack(4), stack(5))
```

```python
import functools
import math

import jax
import jax.numpy as jnp
from jax import lax
from jax.experimental import pallas as pl
from jax.experimental.pallas import tpu as pltpu

F32 = jnp.float32
BF16 = jnp.bfloat16

D_MODEL = 1024
PAGE_SIZE = 128
HEAD_DIM = 64
SB_HEADS = 8
NSA_HEADS = 8
NSA_KV_HEADS = 2
NSA_GROUP = NSA_HEADS // NSA_KV_HEADS
SB_WIDTH = SB_HEADS * HEAD_DIM
NSA_WIDTH = NSA_HEADS * HEAD_DIM
KV_WIDTH = NSA_KV_HEADS * HEAD_DIM
ROPE_DIM = HEAD_DIM // 4
ROPE_THETA = 500000.0
CMP_BLOCK = 32
CMP_STRIDE = 16
CMP_HIDDEN = 2 * HEAD_DIM
SEL_BLOCK = 64
SEL_TOPN = 16
SEL_FORCE = 1e4
WINDOW = 512
PEER_HEADS = 8
PEER_NKEYS = 128
PEER_DK = 256
PEER_TOPK = 16
PEER_SLAB = 512
DEPTH = 2
DN_ALPHA = (2 * DEPTH) ** 0.25
LN_EPS = 1e-5
NEG = -1e30
BELOW_NEG = -3e38
ATT_SCALE = HEAD_DIM ** -0.5
SB_UNDERFLOW = -104.0

LANES = 128
SUBLANES = 8
VMEM_LIMIT = 56 * 1024 * 1024

N_GATE_COLS = 3 * NSA_HEADS
PROJ_COLS = 2816 + 2 * D_MODEL + LANES


def _cparams(sem):
    return pltpu.CompilerParams(dimension_semantics=sem, vmem_limit_bytes=VMEM_LIMIT)


def _nt_dot(a, b):
    return lax.dot_general(a, b, (((1,), (1,)), ((), ())), preferred_element_type=F32)


def _dot(a, b):
    return jnp.dot(a, b, preferred_element_type=F32)


def _gelu(x):
    c = math.sqrt(2.0 / math.pi)
    return 0.5 * x * (1.0 + jnp.tanh(c * (x + 0.044715 * (x * x * x))))


def _sigmoid(x):
    return 1.0 / (1.0 + jnp.exp(-x))


def _softplus(z):
    return jnp.maximum(z, 0.0) + jnp.log1p(jnp.exp(-jnp.abs(z)))


def _layer_norm(y, g, b):
    mu = jnp.mean(y, axis=-1, keepdims=True)
    d = y - mu
    var = jnp.mean(d * d, axis=-1, keepdims=True)
    return d * lax.rsqrt(var + LN_EPS) * g + b


def _proj_kernel(x_ref, w_ref, cos_ref, sa_ref, sb_ref,
                 sbq_ref, sbkv_ref, nq_ref, nkv_ref, win_ref, gab_ref, gates_ref):
    x = x_ref[...].astype(BF16)
    cos, sa, sb = cos_ref[...], sa_ref[...], sb_ref[...]

    def mm(c0, c1):
        return _dot(x, w_ref[:, c0:c1])

    def rope(t):
        half = ROPE_DIM // 2
        return t * cos + pltpu.roll(t, half, 1) * sa + pltpu.roll(t, LANES - half, 1) * sb

    sbq_ref[...] = mm(0, 512)
    sbkv_ref[...] = mm(512, 1536)
    hq = mm(1536, 2048)
    for i in range(4):
        nq_ref[:, i * LANES:(i + 1) * LANES] = rope(hq[:, i * LANES:(i + 1) * LANES])
    hk = mm(2048, 2560)
    nkv_ref[:, 0:128] = rope(hk[:, 0:128])
    nkv_ref[:, 128:256] = hk[:, 128:256]
    nkv_ref[:, 256:384] = rope(hk[:, 256:384])
    nkv_ref[:, 384:512] = hk[:, 384:512]
    hw = mm(2560, 2816)
    win_ref[:, 0:128] = rope(hw[:, 0:128])
    win_ref[:, 128:256] = hw[:, 128:256]
    gab_ref[:, 0:1024] = _sigmoid(mm(2816, 3840))
    gab_ref[:, 1024:2048] = _sigmoid(mm(3840, 4864))
    gates_ref[...] = _sigmoid(mm(4864, 4992))


def _proj(x, w, cos, sa, sb, tm=256):
    nt = x.shape[0]
    row = lambda n: pl.BlockSpec((tm, n), lambda i: (i, 0))
    outs = [(512, F32), (1024, F32), (512, F32), (512, F32), (256, F32), (2048, F32), (128, F32)]
    return pl.pallas_call(
        _proj_kernel,
        grid=(nt // tm,),
        in_specs=[row(D_MODEL), pl.BlockSpec((D_MODEL, PROJ_COLS), lambda i: (0, 0)),
                  row(LANES), row(LANES), row(LANES)],
        out_specs=[row(n) for n, _ in outs],
        out_shape=[jax.ShapeDtypeStruct((nt, n), dt) for n, dt in outs],
        compiler_params=_cparams(("parallel",)),
        name="in_proj",
    )(x, w, cos, sa, sb)


def _sb_weights(z, tri, vis, carry):
    sp = _softplus(z)
    lk = -sp if vis is None else jnp.where(vis, -sp, 0.0)
    hi = lk.astype(BF16)
    lo = (lk - hi.astype(F32)).astype(BF16)
    between = _dot(hi, tri) + _dot(lo, tri) + carry
    w = jnp.exp(z - sp + between)
    if vis is not None:
        w = jnp.where(vis, w, 0.0)
    return w.astype(BF16), carry + jnp.sum(lk, axis=1, keepdims=True)


def _sb_prompt_kernel(q_ref, k_ref, v_ref, tri_ref, o_ref, *, tq):
    i = pl.program_id(2)
    qf = q_ref[...] * ATT_SCALE
    tri = tri_ref[...]
    n_h = LANES // HEAD_DIM
    lane = lax.broadcasted_iota(jnp.int32, (1, LANES), 1)
    in_head = [(lane >= h * HEAD_DIM) & (lane < (h + 1) * HEAD_DIM) for h in range(n_h)]
    qs = [jnp.where(m, qf, 0.0).astype(BF16) for m in in_head]
    qpos = i * tq + lax.broadcasted_iota(jnp.int32, (tq, 1), 0)
    kio = lax.broadcasted_iota(jnp.int32, (1, tq), 1)

    def cond(s):
        return (s[0] <= i) & (s[1] > SB_UNDERFLOW)

    def body(s):
        jj, _, accs, cars = s
        start = pl.multiple_of((i - jj) * tq, tq)
        k = k_ref[pl.ds(start, tq), :].astype(BF16)
        v = v_ref[pl.ds(start, tq), :].astype(BF16)
        vis = (start + kio) < qpos
        new_accs, new_cars = [], []
        for q, acc, car in zip(qs, accs, cars):
            w, car = _sb_weights(_nt_dot(q, k), tri, vis, car)
            new_accs.append(acc + _dot(w, v))
            new_cars.append(car)
        cmax = functools.reduce(jnp.maximum, [jnp.max(c) for c in new_cars])
        return jj + 1, cmax, tuple(new_accs), tuple(new_cars)

    init = (jnp.int32(0), jnp.float32(0.0),
            tuple(jnp.zeros((tq, LANES), F32) for _ in range(n_h)),
            tuple(jnp.zeros((tq, 1), F32) for _ in range(n_h)))
    accs = lax.while_loop(cond, body, init)[2]
    out = accs[0]
    for m, acc in zip(in_head[1:], accs[1:]):
        out = jnp.where(m, acc, out)
    o_ref[...] = out


def _strict_lower(n):
    r = lax.broadcasted_iota(jnp.int32, (n, n), 0)
    c = lax.broadcasted_iota(jnp.int32, (n, n), 1)
    return (r > c).astype(BF16)


def _sb_prompt(sbq, sbkv, B, L, tq=256):
    tq = min(tq, L)
    nq = L // tq
    pairs = SB_WIDTH // LANES
    blk = lambda b, h, i: (b * nq + i, h)
    return pl.pallas_call(
        functools.partial(_sb_prompt_kernel, tq=tq),
        grid=(B, pairs, nq),
        in_specs=[pl.BlockSpec((tq, LANES), blk),
                  pl.BlockSpec((L, LANES), lambda b, h, i: (b, h)),
                  pl.BlockSpec((L, LANES), lambda b, h, i: (b, pairs + h)),
                  pl.BlockSpec((tq, tq), lambda b, h, i: (0, 0))],
        out_specs=pl.BlockSpec((tq, LANES), blk),
        out_shape=jax.ShapeDtypeStruct((B * L, SB_WIDTH), F32),
        compiler_params=_cparams(("parallel", "parallel", "arbitrary")),
        name="sb_prompt",
    )(sbq, sbkv, sbkv, _strict_lower(tq))


def _sb_decode_kernel(pt_ref, q_ref, new_ref, cache_ref, tri_ref, o_ref, buf_ref, sem_ref,
                      *, layer, n_pages, lq):
    b = pl.program_id(0)
    rows = SB_HEADS * lq
    past = n_pages * PAGE_SIZE
    tri = tri_ref[...]
    qs = [(q_ref[0, h] * ATT_SCALE).astype(BF16) for h in range(SB_HEADS)]
    qpos = past + lax.broadcasted_iota(jnp.int32, (rows, 1), 0) % lq
    kio = lax.broadcasted_iota(jnp.int32, (1, PAGE_SIZE), 1)

    def page_copy(p, slot):
        return pltpu.make_async_copy(cache_ref.at[layer, pt_ref[b, p]], buf_ref.at[slot],
                                     sem_ref.at[slot])

    def block(kv, vis, car):
        z = jnp.concatenate([_dot(qs[h], kv(0, h).astype(BF16)) for h in range(SB_HEADS)], axis=0)
        w, car = _sb_weights(z, tri, vis, car)
        contrib = jnp.concatenate(
            [_nt_dot(w[h * lq:(h + 1) * lq], kv(1, h).astype(BF16)) for h in range(SB_HEADS)],
            axis=0)
        return contrib, car

    page_copy(n_pages - 1, 0).start()
    acc, car = block(lambda c, h: new_ref[0, c, h], (past + kio) < qpos,
                     jnp.zeros((rows, 1), F32))

    def cond(s):
        return (s[0] < n_pages) & (s[3] > SB_UNDERFLOW)

    def body(s):
        jj, acc, car, _ = s
        slot = jj % 2
        p = n_pages - 1 - jj
        page_copy(p, slot).wait()

        @pl.when(jj + 1 < n_pages)
        def _():
            page_copy(p - 1, 1 - slot).start()

        contrib, car = block(lambda c, h: buf_ref[slot, c, h], None, car)
        return jj + 1, acc + contrib, car, jnp.max(car)

    jj, acc, _, _ = lax.while_loop(cond, body, (jnp.int32(0), acc, car, jnp.max(car)))

    @pl.when(jj < n_pages)
    def _():
        page_copy(n_pages - 1 - jj, jj % 2).wait()

    o_ref[0] = acc


def _sb_decode(page_table, q, kv_new, cache, layer):
    bd, n_pages = page_table.shape
    lq = q.shape[2]
    rows = SB_HEADS * lq
    page_shape = (2, SB_HEADS, HEAD_DIM, PAGE_SIZE)
    grid_spec = pltpu.PrefetchScalarGridSpec(
        num_scalar_prefetch=1,
        grid=(bd,),
        in_specs=[pl.BlockSpec((1, SB_HEADS, lq, HEAD_DIM), lambda b, pt: (b, 0, 0, 0)),
                  pl.BlockSpec((1,) + page_shape, lambda b, pt: (b, 0, 0, 0, 0)),
                  pl.BlockSpec(memory_space=pl.ANY),
                  pl.BlockSpec((PAGE_SIZE, PAGE_SIZE), lambda b, pt: (0, 0))],
        out_specs=pl.BlockSpec((1, rows, HEAD_DIM), lambda b, pt: (b, 0, 0)),
        scratch_shapes=[pltpu.VMEM((2,) + page_shape, F32), pltpu.SemaphoreType.DMA((2,))],
    )
    return pl.pallas_call(
        functools.partial(_sb_decode_kernel, layer=layer, n_pages=n_pages, lq=lq),
        grid_spec=grid_spec,
        out_shape=jax.ShapeDtypeStruct((bd, rows, HEAD_DIM), F32),
        compiler_params=_cparams(("arbitrary",)),
        name="sb_decode",
    )(page_table, q, kv_new, cache, _strict_lower(PAGE_SIZE))


def _cmp_partial_kernel(rk_ref, rv_ref, wa_ref, wb_ref, pea_ref, peb_ref, o_ref, xk_ref, xv_ref,
                        tk_ref, tv_ref, *, pg, transposed):
    nchunk = PAGE_SIZE // CMP_STRIDE

    def gather(p, c):
        dst = pl.ds(pl.multiple_of(p * nchunk, nchunk), nchunk)
        if transposed:
            tk_ref[...] = rk_ref[p].T
            tv_ref[...] = rv_ref[p].T
        for j in range(CMP_STRIDE):
            src = pl.ds(j, nchunk, stride=CMP_STRIDE)
            xk_ref[dst, j * LANES:(j + 1) * LANES] = tk_ref[src, :] if transposed else rk_ref[p, src, :]
            xv_ref[dst, j * LANES:(j + 1) * LANES] = tv_ref[src, :] if transposed else rv_ref[p, src, :]
        return c

    lax.fori_loop(0, pg, gather, 0)
    for t, x_ref in enumerate((xk_ref, xv_ref)):
        x = x_ref[...]
        a = _dot((x + pea_ref[t]).astype(BF16), wa_ref[t])
        b = _dot((x + peb_ref[t]).astype(BF16), wb_ref[t])
        o_ref[:, t * 512:t * 512 + 256] = a
        o_ref[:, t * 512 + 256:(t + 1) * 512] = b


def _cmp_partial(rows, wa, wb, pea, peb, pg, n_pages, page0=0, transposed=False):
    nchunk = PAGE_SIZE // CMP_STRIDE
    kdim = CMP_STRIDE * LANES
    blk0 = page0 // pg
    const3 = lambda s: pl.BlockSpec(s, lambda i: (0, 0, 0))
    part = lambda f: pl.BlockSpec((pg, PAGE_SIZE, LANES),
                                  lambda i: (blk0 + i, f, 0) if transposed else (blk0 + i, 0, f))
    return pl.pallas_call(
        functools.partial(_cmp_partial_kernel, pg=pg, transposed=transposed),
        grid=(n_pages // pg,),
        in_specs=[part(0), part(1),
                  const3((2, kdim, 2 * LANES)), const3((2, kdim, 2 * LANES)),
                  const3((2, 1, kdim)), const3((2, 1, kdim))],
        out_specs=pl.BlockSpec((pg * nchunk, 1024), lambda i: (i, 0)),
        out_shape=jax.ShapeDtypeStruct((n_pages * nchunk, 1024), F32),
        scratch_shapes=[pltpu.VMEM((pg * nchunk, kdim), F32), pltpu.VMEM((pg * nchunk, kdim), F32),
                        pltpu.VMEM((PAGE_SIZE, LANES), F32), pltpu.VMEM((PAGE_SIZE, LANES), F32)],
        compiler_params=_cparams(("parallel",)),
        name="cmp_partial",
    )(rows, rows, wa, wb, pea, peb)


def _cmp_finish_kernel(pt_ref, ab_ref, w2_ref, ck_ref, cv_ref, buf_ref, sem_ref, *, pp):
    b = pl.program_id(0)
    nchunk = PAGE_SIZE // CMP_STRIDE

    def copy(j):
        return pltpu.make_async_copy(ab_ref.at[pt_ref[b, j]], buf_ref.at[j], sem_ref.at[j])

    for j in range(pp):
        copy(j).start()
    for j in range(pp):
        copy(j).wait()

    n = pp * nchunk
    x = buf_ref[...].reshape(n, 1024)
    last = lax.broadcasted_iota(jnp.int32, (n, 1), 0) == n - 1
    for t, o_ref in enumerate((ck_ref, cv_ref)):
        a = x[:, t * 512:t * 512 + 256]
        bnext = pltpu.roll(x[:, t * 512 + 256:(t + 1) * 512], n - 1, 0)
        hid = _gelu(a + jnp.where(last, 0.0, bnext))
        o_ref[0] = _dot(hid.astype(BF16), w2_ref[t])


def _cmp_finish(page_table, ab, w2big):
    nb, pp = page_table.shape
    nchunk = PAGE_SIZE // CMP_STRIDE
    n = pp * nchunk
    grid_spec = pltpu.PrefetchScalarGridSpec(
        num_scalar_prefetch=1,
        grid=(nb,),
        in_specs=[pl.BlockSpec(memory_space=pl.ANY),
                  pl.BlockSpec((2, 2 * LANES, LANES), lambda b, pt: (0, 0, 0))],
        out_specs=[pl.BlockSpec((1, n, LANES), lambda b, pt: (b, 0, 0))] * 2,
        scratch_shapes=[pltpu.VMEM((pp, nchunk, 1024), F32), pltpu.SemaphoreType.DMA((pp,))],
    )
    return pl.pallas_call(
        functools.partial(_cmp_finish_kernel, pp=pp),
        grid_spec=grid_spec,
        out_shape=[jax.ShapeDtypeStruct((nb, n, LANES), F32)] * 2,
        compiler_params=_cparams(("arbitrary",)),
        name="cmp_finish",
    )(page_table, ab, w2big)


def _topk_mask(score, n_top, axis):
    idx_f = lax.broadcasted_iota(jnp.int32, score.shape, axis).astype(F32)
    sel = jnp.zeros(score.shape, F32)
    work = score
    for _ in range(n_top):
        m = jnp.max(work, axis=axis, keepdims=True)
        first = jnp.min(jnp.where(work == m, idx_f, float(score.shape[axis])), axis=axis,
                        keepdims=True)
        hit = idx_f == first
        sel = jnp.where(hit & (m > 0.5 * NEG), 1.0, sel)
        work = jnp.where(hit, BELOW_NEG, work)
    return sel


def _cmp_branch(q, ck, cv, qpos):
    sc = _nt_dot(q, ck.astype(BF16))
    nio = lax.broadcasted_iota(jnp.int32, (1, sc.shape[1]), 1)
    cvis = (CMP_STRIDE * nio + (CMP_BLOCK - 1)) <= qpos
    scm = jnp.where(cvis, sc, NEG)
    e = jnp.exp(scm - jnp.max(scm, axis=-1, keepdims=True))
    pc = jnp.where(cvis, e / jnp.sum(e, axis=-1, keepdims=True), 0.0).astype(BF16)
    return _dot(pc, cv.astype(BF16)), pc


def _select_blocks(imp, qpos, n_top, axis):
    shape = (1, imp.shape[1]) if axis == 1 else (imp.shape[0], 1)
    sio = lax.broadcasted_iota(jnp.int32, shape, axis)
    cur = qpos // SEL_BLOCK
    forced = (sio == 0) | (sio == cur) | (sio == cur - 1)
    imp = jnp.where(sio <= cur, imp + jnp.where(forced, SEL_FORCE, 0.0), NEG)
    return _topk_mask(imp, n_top, axis)


def _softmax_step(state, s, mask, v, every_row_seen=False):
    m, l, acc = state
    sm = jnp.where(mask, s, NEG)
    m_new = jnp.maximum(m, jnp.max(sm, axis=-1, keepdims=True))
    alpha = jnp.exp(m - m_new)
    p = jnp.exp(sm - m_new)
    if not every_row_seen:
        p = jnp.where(mask, p, 0.0)
    return (m_new, alpha * l + jnp.sum(p, axis=-1, keepdims=True),
            alpha * acc + _dot(p.astype(BF16), v))


def _softmax_init(r):
    return (jnp.full((r, 1), NEG, F32), jnp.zeros((r, 1), F32), jnp.zeros((r, LANES), F32))


def _nsa_prompt_kernel(q_ref, ck_ref, cv_ref, ks_ref, vs_ref, kw_ref, vw_ref, g_ref, ovt_ref,
                       o_ref, *, tq, n_top):
    i = pl.program_id(2)
    r = NSA_GROUP * tq
    q = (q_ref[0, 0].reshape(r, LANES) * ATT_SCALE).astype(BF16)
    qpos = i * tq + lax.broadcasted_iota(jnp.int32, (r, 1), 0) % tq

    o_c, pc = _cmp_branch(q, ck_ref[0], cv_ref[0], qpos)
    ovt = ovt_ref[...]
    imp_t = _nt_dot(ovt, pc[0:tq])
    for g in range(1, NSA_GROUP):
        imp_t = imp_t + _nt_dot(ovt, pc[g * tq:(g + 1) * tq])
    qpos_t = i * tq + lax.broadcasted_iota(jnp.int32, (1, tq), 1)
    sel = _select_blocks(imp_t, qpos_t, n_top, 0).T.astype(BF16)
    ns = sel.shape[1]

    tk = 2 * tq
    per = tk // SEL_BLOCK
    sio = lax.broadcasted_iota(jnp.int32, (ns, tk), 0)
    kio_s = lax.broadcasted_iota(jnp.int32, (ns, tk), 1) // SEL_BLOCK
    kio = lax.broadcasted_iota(jnp.int32, (1, tk), 1)

    def sel_body(j, state):
        start = pl.multiple_of(j * tk, tk)
        k = ks_ref[0, pl.ds(start, tk), :].astype(BF16)
        v = vs_ref[0, pl.ds(start, tk), :].astype(BF16)
        expand = (sio == kio_s + j * per).astype(BF16)
        picked = _dot(sel, expand)
        picked = jnp.concatenate([picked] * NSA_GROUP, axis=0)
        mask = (picked > 0.5) & ((start + kio) <= qpos)
        return _softmax_step(state, _nt_dot(q, k), mask, v, every_row_seen=True)

    n_sel_steps = (i * tq + tq - 1) // tk + 1
    m_s, l_s, acc_s = lax.fori_loop(0, n_sel_steps, sel_body, _softmax_init(r))

    wio = lax.broadcasted_iota(jnp.int32, (1, tq), 1)

    def win_body(jj, state):
        start = pl.multiple_of((i - jj) * tq, tq)
        k = kw_ref[0, pl.ds(start, tq), :].astype(BF16)
        v = vw_ref[0, pl.ds(start, tq), :].astype(BF16)
        rel = qpos - (start + wio)
        mask = (rel >= 0) & (rel < WINDOW)
        return _softmax_step(state, _nt_dot(q, k), mask, v, every_row_seen=True)

    m_w, l_w, acc_w = lax.fori_loop(0, jnp.minimum(i, WINDOW // tq) + 1, win_body,
                                    _softmax_init(r))

    g = g_ref[0, 0].reshape(r, 3)
    o = o_c * g[:, 0:1] + (acc_s / l_s) * g[:, 1:2] + (acc_w / l_w) * g[:, 2:3]
    o_ref[0, 0] = o.reshape(NSA_GROUP, tq, LANES)


def _nsa_prompt(qm, ck, cv, nkv, win, gates, ov, n_top, tq=128):
    B, _, _, L, _ = qm.shape
    nc = ck.shape[1]
    seq = lambda lane_blk: pl.BlockSpec((1, L, LANES), lambda b, h, i: (b, 0, lane_blk))
    qspec = lambda last: pl.BlockSpec((1, 1, NSA_GROUP, tq, last), lambda b, h, i: (b, h, 0, i, 0))
    cspec = pl.BlockSpec((1, nc, LANES), lambda b, h, i: (b, 0, 0))
    return pl.pallas_call(
        functools.partial(_nsa_prompt_kernel, tq=tq, n_top=n_top),
        grid=(B, NSA_KV_HEADS, L // tq),
        in_specs=[qspec(LANES), cspec, cspec, seq(2), seq(3), seq(0), seq(1), qspec(3),
                  pl.BlockSpec(ov.shape, lambda b, h, i: (0, 0))],
        out_specs=qspec(LANES),
        out_shape=jax.ShapeDtypeStruct(qm.shape, F32),
        compiler_params=_cparams(("parallel", "parallel", "arbitrary")),
        name="nsa_prompt",
    )(qm, ck, cv, nkv, nkv, win, win, gates, ov)


def _nsa_decode_kernel(pt_ref, q_ref, ck_ref, cv_ref, kn_ref, vn_ref, cache_ref, kw_ref, vw_ref,
                       g_ref, ov_ref, ex_ref, o_ref, buf_ref, sem_ref,
                       *, page0, n_pages, lq, n_top):
    b = pl.program_id(0)
    r = NSA_HEADS * lq
    rk = NSA_KV_HEADS * lq
    past = n_pages * PAGE_SIZE

    def page_copy(j):
        return pltpu.make_async_copy(
            cache_ref.at[page0 + pt_ref[b, j], pl.ds(2 * LANES, 2 * LANES), :],
            buf_ref.at[j], sem_ref.at[j])

    for j in range(n_pages):
        page_copy(j).start()

    q = (q_ref[0] * ATT_SCALE).astype(BF16)
    qpos = past + lax.broadcasted_iota(jnp.int32, (r, 1), 0) % lq
    kio = lax.broadcasted_iota(jnp.int32, (1, PAGE_SIZE), 1)

    o_c, pc = _cmp_branch(q, ck_ref[0], cv_ref[0], qpos)
    impg = _dot(pc, ov_ref[...])
    ns = impg.shape[1]
    impg = impg.reshape(NSA_KV_HEADS, NSA_GROUP, lq, ns)
    imp = impg[:, 0]
    for g in range(1, NSA_GROUP):
        imp = imp + impg[:, g]
    qpos1 = past + lax.broadcasted_iota(jnp.int32, (rk, 1), 0) % lq
    sel = _select_blocks(imp.reshape(rk, ns), qpos1, n_top, 1)
    nk = past + PAGE_SIZE
    picked = _dot(sel.astype(BF16), ex_ref[...])
    picked = jnp.broadcast_to(picked.reshape(NSA_KV_HEADS, 1, lq, nk),
                              (NSA_KV_HEADS, NSA_GROUP, lq, nk)).reshape(r, nk)
    kpos = lax.broadcasted_iota(jnp.int32, (1, nk), 1)
    mask = (picked > 0.5) & (kpos <= qpos)

    for j in range(n_pages):
        page_copy(j).wait()
    s = jnp.concatenate([_dot(q, buf_ref[j, 0:LANES, :].astype(BF16)) for j in range(n_pages)]
                        + [_nt_dot(q, kn_ref[0].astype(BF16))], axis=1)
    sm = jnp.where(mask, s, NEG)
    p = jnp.where(mask, jnp.exp(sm - jnp.max(sm, axis=-1, keepdims=True)), 0.0)
    l_s = jnp.sum(p, axis=-1, keepdims=True)
    p = p.astype(BF16)
    acc_s = _dot(p[:, past:], vn_ref[0].astype(BF16))
    for j in range(n_pages):
        acc_s = acc_s + _nt_dot(p[:, j * PAGE_SIZE:(j + 1) * PAGE_SIZE],
                                buf_ref[j, LANES:2 * LANES, :].astype(BF16))

    state = _softmax_init(r)
    for w in range(kw_ref.shape[1] // PAGE_SIZE):
        k = kw_ref[0, w * PAGE_SIZE:(w + 1) * PAGE_SIZE, :].astype(BF16)
        v = vw_ref[0, w * PAGE_SIZE:(w + 1) * PAGE_SIZE, :].astype(BF16)
        rel = qpos - (past - WINDOW + w * PAGE_SIZE + kio)
        state = _softmax_step(state, _nt_dot(q, k), (rel >= 0) & (rel < WINDOW), v)
    _, l_w, acc_w = state
    g = g_ref[0]
    o_ref[0] = o_c * g[:, 0:1] + (acc_s / l_s) * g[:, 1:2] + (acc_w / l_w) * g[:, 2:3]


def _nsa_decode(page_table, qm, ck, cv, k_new, v_new, cache, kwin, vwin, gates, ov, ex,
                page0, lq, n_top):
    bd, n_pages = page_table.shape
    r = NSA_HEADS * lq
    per_b = lambda a: pl.BlockSpec((1,) + a.shape[1:], lambda b, pt: (b, 0, 0))
    const = lambda a: pl.BlockSpec(a.shape, lambda b, pt: (0, 0))
    grid_spec = pltpu.PrefetchScalarGridSpec(
        num_scalar_prefetch=1,
        grid=(bd,),
        in_specs=[per_b(qm), per_b(ck), per_b(cv), per_b(k_new), per_b(v_new),
                  pl.BlockSpec(memory_space=pl.ANY),
                  per_b(kwin), per_b(vwin), per_b(gates), const(ov), const(ex)],
        out_specs=pl.BlockSpec((1, r, LANES), lambda b, pt: (b, 0, 0)),
        scratch_shapes=[pltpu.VMEM((n_pages, 2 * LANES, PAGE_SIZE), F32),
                        pltpu.SemaphoreType.DMA((n_pages,))],
    )
    return pl.pallas_call(
        functools.partial(_nsa_decode_kernel, page0=page0, n_pages=n_pages, lq=lq, n_top=n_top),
        grid_spec=grid_spec,
        out_shape=jax.ShapeDtypeStruct((bd, r, LANES), F32),
        compiler_params=_cparams(("arbitrary",)),
        name="nsa_decode",
    )(page_table, qm, ck, cv, k_new, v_new, cache, kwin, vwin, gates, ov, ex)


def _merge_kernel(x_ref, oa_ref, ob_ref, gab_ref, wa_ref, wb_ref, wo_ref, g_ref, b_ref, y_ref):
    a = _dot(oa_ref[...].astype(BF16), wa_ref[...])
    b = _dot(ob_ref[...].astype(BF16), wb_ref[...])
    merged = gab_ref[:, 0:D_MODEL] * a + gab_ref[:, D_MODEL:2 * D_MODEL] * b
    y = DN_ALPHA * x_ref[...] + _dot(merged.astype(BF16), wo_ref[...])
    y_ref[...] = _layer_norm(y, g_ref[...], b_ref[...])


def _merge(x, oa, ob, gab, wa, wb, wo, g, b, tm=256):
    nt = x.shape[0]
    row = lambda n: pl.BlockSpec((tm, n), lambda i: (i, 0))
    const = lambda a: pl.BlockSpec(a.shape, lambda i: (0, 0))
    return pl.pallas_call(
        _merge_kernel,
        grid=(nt // tm,),
        in_specs=[row(D_MODEL), row(SB_WIDTH), row(NSA_WIDTH), row(2 * D_MODEL),
                  const(wa), const(wb), const(wo), const(g), const(b)],
        out_specs=row(D_MODEL),
        out_shape=jax.ShapeDtypeStruct((nt, D_MODEL), F32),
        compiler_params=_cparams(("parallel",)),
        name="merge_ln1",
    )(x, oa, ob, gab, wa, wb, wo, g, b)


def _top_values(x, k, exact, n_pad=0):
    n = x.shape[0]
    vals = []
    if exact:
        rio = lax.broadcasted_iota(jnp.int32, x.shape, 0).astype(F32)
        for _ in range(k):
            m = jnp.max(x, axis=0, keepdims=True)
            idx = jnp.min(jnp.where(x == m, rio, float(n)), axis=0, keepdims=True)
            x = jnp.where(rio == idx, BELOW_NEG, x)
            vals.append(m)
        return vals, jnp.float32(0.0)
    for _ in range(k):
        m = jnp.max(x, axis=0, keepdims=True)
        x = jnp.where(x == m, BELOW_NEG, x)
        vals.append(m)
    removed = jnp.sum(jnp.where(x == BELOW_NEG, 1.0, 0.0), axis=0, keepdims=True)
    return vals, jnp.max(jnp.abs(removed - float(k + n_pad)))


def _pair_products(a, b):
    rows = [a[i] * b[j] for i in range(len(a)) for j in range(len(b))
            if (i + 1) * (j + 1) <= PEER_TOPK]
    pad = (-len(rows)) % SUBLANES
    rows += [jnp.full_like(rows[0], BELOW_NEG)] * pad
    return jnp.concatenate(rows, axis=0), pad


def _peer_route_kernel(x_ref, wq_ref, keys_ref, as_ref, b_ref, thr_ref):
    x = x_ref[...].astype(BF16)
    half = PEER_DK // 2

    def route(exact):
        ties = jnp.float32(0.0)
        for h in range(PEER_HEADS):
            qh = _dot(x, wq_ref[:, h * PEER_DK:(h + 1) * PEER_DK]).astype(BF16)
            s1 = _nt_dot(keys_ref[h, 0], qh[:, :half])
            s2 = _nt_dot(keys_ref[h, 1], qh[:, half:])
            t1, tie1 = _top_values(s1, PEER_TOPK, exact)
            t2, tie2 = _top_values(s2, PEER_TOPK, exact)
            a_un = jnp.exp(s1 - t1[0])
            b = jnp.exp(s2 - t2[0])
            a_top = [jnp.exp(v - t1[0]) for v in t1]
            b_top = [jnp.exp(v - t2[0]) for v in t2]
            prod, pad = _pair_products(a_top, b_top)
            zs, tie3 = _top_values(prod, PEER_TOPK, exact, pad)
            zinv = 1.0 / sum(zs)
            prod, pad = _pair_products([v * zinv for v in a_top], b_top)
            thrs, tie4 = _top_values(prod, PEER_TOPK, exact, pad)
            as_ref[h] = a_un * zinv
            b_ref[h] = b
            thr_ref[h] = thrs[-1]
            ties = functools.reduce(jnp.maximum, [ties, tie1, tie2, tie3, tie4])
        return ties

    ties = route(exact=False)

    @pl.when(ties > 0.0)
    def _():
        route(exact=True)


def _peer_route(x, wq, keys, tt):
    nt = x.shape[0]
    hspec = lambda n: pl.BlockSpec((PEER_HEADS, n, tt), lambda i: (0, 0, i))
    return pl.pallas_call(
        _peer_route_kernel,
        grid=(nt // tt,),
        in_specs=[pl.BlockSpec((tt, D_MODEL), lambda i: (i, 0)),
                  pl.BlockSpec(wq.shape, lambda i: (0, 0)),
                  pl.BlockSpec(keys.shape, lambda i: (0, 0, 0, 0))],
        out_specs=[hspec(PEER_NKEYS), hspec(PEER_NKEYS), hspec(1)],
        out_shape=[jax.ShapeDtypeStruct((PEER_HEADS, PEER_NKEYS, nt), F32),
                   jax.ShapeDtypeStruct((PEER_HEADS, PEER_NKEYS, nt), F32),
                   jax.ShapeDtypeStruct((PEER_HEADS, 1, nt), F32)],
        compiler_params=_cparams(("parallel",)),
        name="peer_route",
    )(x, wq, keys)


def _peer_dense_kernel(x_ref, as_ref, b_ref, thr_ref, *refs, rows_per_step, n_split):
    u_refs, vt_refs = refs[:n_split], refs[n_split:2 * n_split]
    g_ref, bb_ref, y_ref, acc_ref, act_ref, gate_ref = refs[2 * n_split:]
    c = pl.program_id(1)

    @pl.when(c == 0)
    def _():
        acc_ref[...] = jnp.zeros_like(acc_ref)

    xb = x_ref[...].astype(BF16)
    tt = xb.shape[0]
    rows_g = rows_per_step // n_split

    def first_matmul(s):
        act_ref[s % 2] = _nt_dot(u_refs[s][...], xb)

    first_matmul(0)
    for s in range(n_split):
        slot = s % 2
        if s + 1 < n_split:
            first_matmul(s + 1)
        for r in range(rows_g):
            row = s * rows_g + r
            base = pl.multiple_of(c * rows_per_step + row // SUBLANES * SUBLANES, SUBLANES)
            blk = slice(r * PEER_NKEYS, (r + 1) * PEER_NKEYS)
            for lg in range(tt // LANES):
                ls = slice(lg * LANES, (lg + 1) * LANES)
                w = None
                for h in range(PEER_HEADS):
                    a8 = as_ref[h, pl.ds(base, SUBLANES), ls]
                    p = b_ref[h, :, ls] * a8[row % SUBLANES:row % SUBLANES + 1, :]
                    wh = jnp.where(p >= thr_ref[h, :, ls], p, 0.0)
                    w = wh if w is None else w + wh
                gate_ref[slot, blk, ls] = (w * _gelu(act_ref[slot, blk, ls])).astype(BF16)
        acc_ref[...] += _dot(vt_refs[s][...], gate_ref[slot])

    @pl.when(c == pl.num_programs(1) - 1)
    def _():
        y = DN_ALPHA * x_ref[...] + acc_ref[...].T
        y_ref[...] = _layer_norm(y, g_ref[...], bb_ref[...])


def _peer_dense(x, a_s, b, thr, u, vt, g, bb, tt, n_split=4):
    nt = x.shape[0]
    slab = vt.shape[2]
    ne = n_split * slab
    n_exp = u.shape[0]
    hspec = lambda n: pl.BlockSpec((PEER_HEADS, n, tt), lambda i, c: (0, 0, i))
    const = lambda a: pl.BlockSpec(a.shape, lambda i, c: (0, 0))
    u_spec = lambda k: pl.BlockSpec((slab, D_MODEL), lambda i, c: (c * n_split + k, 0))
    vt_spec = lambda k: pl.BlockSpec((None, D_MODEL, slab), lambda i, c: (c * n_split + k, 0, 0))
    return pl.pallas_call(
        functools.partial(_peer_dense_kernel, rows_per_step=ne // PEER_NKEYS, n_split=n_split),
        grid=(nt // tt, n_exp // ne),
        in_specs=[pl.BlockSpec((tt, D_MODEL), lambda i, c: (i, 0)),
                  hspec(PEER_NKEYS), hspec(PEER_NKEYS), hspec(1)]
                 + [u_spec(k) for k in range(n_split)] + [vt_spec(k) for k in range(n_split)]
                 + [const(g), const(bb)],
        out_specs=pl.BlockSpec((tt, D_MODEL), lambda i, c: (i, 0)),
        out_shape=jax.ShapeDtypeStruct((nt, D_MODEL), F32),
        scratch_shapes=[pltpu.VMEM((D_MODEL, tt), F32),
                        pltpu.VMEM((2, slab, tt), F32),
                        pltpu.VMEM((2, slab, tt), BF16)],
        compiler_params=_cparams(("parallel", "arbitrary")),
        name="peer_dense",
    )(x, a_s, b, thr, *([u] * n_split), *([vt] * n_split), g, bb)


def _cast_kernel(x_ref, o_ref, *, transpose):
    if transpose:
        o_ref[0, 0] = x_ref[0].T.astype(BF16)
    else:
        o_ref[0] = x_ref[0].astype(BF16)


def _expert_table_bf16(w, transpose, tr=PEER_SLAB):
    depth, n_exp, d = w.shape
    if transpose:
        out_shape = (depth, n_exp // tr, d, tr)
        out_spec = pl.BlockSpec((1, 1, d, tr), lambda l, i: (l, i, 0, 0))
    else:
        out_shape, out_spec = (depth, n_exp, d), pl.BlockSpec((1, tr, d), lambda l, i: (l, i, 0))
    return pl.pallas_call(
        functools.partial(_cast_kernel, transpose=transpose),
        grid=(depth, n_exp // tr),
        in_specs=[pl.BlockSpec((1, tr, d), lambda l, i: (l, i, 0))],
        out_specs=out_spec,
        out_shape=jax.ShapeDtypeStruct(out_shape, BF16),
        compiler_params=_cparams(("parallel", "parallel")),
        name="expert_table_bf16",
    )(w)


def _rope_tables(pos):
    half = ROPE_DIM // 2
    inv_freq = ROPE_THETA ** (-jnp.arange(half, dtype=F32) * (2.0 / ROPE_DIM))
    ang = pos.astype(F32)[:, None] * inv_freq[None, :]
    cos, sin = jnp.cos(ang), jnp.sin(ang)
    n = pos.shape[0]
    ones = jnp.ones((n, HEAD_DIM - ROPE_DIM), F32)
    zeros = jnp.zeros((n, HEAD_DIM - ROPE_DIM), F32)
    zh = jnp.zeros((n, half), F32)
    c = jnp.concatenate([cos, cos, ones], axis=1)
    sa = jnp.concatenate([zh, sin, zeros], axis=1)
    sb = jnp.concatenate([-sin, zh, zeros], axis=1)
    two = lambda t: jnp.concatenate([t, t], axis=1)
    return two(c), two(sa), two(sb)


def _overlap(n_cmp_pad, n_sel_pad, n_cmp, n_sel):
    c0 = (CMP_STRIDE * jnp.arange(n_cmp_pad))[:, None]
    s0 = (SEL_BLOCK * jnp.arange(n_sel_pad))[None, :]
    ov = jnp.clip(jnp.minimum(c0 + CMP_BLOCK, s0 + SEL_BLOCK) - jnp.maximum(c0, s0), 0)
    ov = ov.astype(F32) * (1.0 / CMP_BLOCK)
    valid = (jnp.arange(n_cmp_pad) < n_cmp)[:, None] & (jnp.arange(n_sel_pad) < n_sel)[None, :]
    return jnp.where(valid, ov, 0.0).astype(BF16)


def _pad_to(n, m):
    return -(-n // m) * m


def _largest_divisor(n, cap):
    return max(d for d in range(1, cap + 1) if n % d == 0)


def _head_masked(t, n_heads):
    eye = jnp.eye(n_heads, dtype=t.dtype)
    out = t[..., :, :, None, :] * eye[:, None, :, None]
    return out.reshape(t.shape[:-1] + (n_heads * HEAD_DIM,))


def _layer(l, xcat, geom, cache_sb, cache_nsa, state_win, page_table, tables, W):
    B, L, Bd, Lq, n_pages = geom
    past = n_pages * PAGE_SIZE
    Np, Ns = B * L, Bd * Lq
    NT = xcat.shape[0]
    cos, sa, sb = tables

    sbq, sbkv, nq, nkv, win, gab, gates = _proj(xcat, W["w_in"][l], cos, sa, sb)
    sl_p = lambda a: a[:Np]
    sl_s = lambda a: a[Np:Np + Ns]

    new_sb_p = sl_p(sbkv).reshape(B, L, 2, SB_HEADS, HEAD_DIM)
    new_sb_s = sl_s(sbkv).reshape(Bd, Lq, 2, SB_HEADS, HEAD_DIM)
    new_nsa_p = sl_p(nkv).reshape(B, L, 4, NSA_KV_HEADS, HEAD_DIM)
    new_nsa_s = sl_s(nkv).reshape(Bd, Lq, 4, NSA_KV_HEADS, HEAD_DIM)
    win_p = sl_p(win).reshape(B, L, 2, NSA_KV_HEADS, HEAD_DIM)
    win_all_s = jnp.concatenate(
        [state_win[l], sl_s(win).reshape(Bd, Lq, 2, NSA_KV_HEADS, HEAD_DIM)], axis=1)

    o_sb_p = _sb_prompt(sbq, sbkv, B, L)

    n_pool = cache_sb.shape[1]
    qd = sl_s(sbq).reshape(Bd, Lq, SB_HEADS, HEAD_DIM).transpose(0, 2, 1, 3)
    pad_rows = lambda t: jnp.pad(t, ((0, 0), (0, PAGE_SIZE - Lq)) + ((0, 0),) * (t.ndim - 2))
    rows_minor = lambda t: jnp.moveaxis(t, -4, -1)
    o = _sb_decode(page_table, qd, rows_minor(pad_rows(new_sb_s)), rows_minor(cache_sb), l)
    o_sb_s = o.reshape(Bd, SB_HEADS, Lq, HEAD_DIM).transpose(0, 2, 1, 3).reshape(Ns, SB_WIDTH)

    cw = W["cmp"][l]
    ab = _cmp_partial(nkv.reshape(NT // PAGE_SIZE, PAGE_SIZE, 4 * LANES), *cw[:4],
                      pg=_largest_divisor(NT // PAGE_SIZE, 33), n_pages=NT // PAGE_SIZE)
    ident = jnp.arange(Np // PAGE_SIZE, dtype=jnp.int32).reshape(B, L // PAGE_SIZE)
    ck_p, cv_p = _cmp_finish(ident, ab.reshape(NT // PAGE_SIZE, -1, 1024), cw[4])
    nsa_pages = rows_minor(cache_nsa).reshape(-1, 4 * LANES, PAGE_SIZE)
    ab = _cmp_partial(nsa_pages, *cw[:4], pg=_largest_divisor(n_pool, 32), n_pages=n_pool,
                      page0=l * n_pool, transposed=True)
    ck_s, cv_s = _cmp_finish(page_table, ab.reshape(n_pool, -1, 1024), cw[4])

    n_cmp_p = (L - CMP_BLOCK) // CMP_STRIDE + 1
    n_sel_p = -(-L // SEL_BLOCK)
    ov_p = _overlap(ck_p.shape[1], _pad_to(n_sel_p, LANES), n_cmp_p, n_sel_p).T
    qg = sl_p(nq).reshape(B, L, NSA_KV_HEADS, NSA_GROUP, HEAD_DIM).transpose(0, 2, 3, 1, 4)
    qm = _head_masked(qg.transpose(0, 2, 1, 3, 4), NSA_KV_HEADS).transpose(0, 2, 1, 3, 4)
    gt = sl_p(gates)[:, :N_GATE_COLS].reshape(B, L, NSA_KV_HEADS, NSA_GROUP, 3).transpose(0, 2, 3, 1, 4)
    o = _nsa_prompt(qm, ck_p, cv_p, sl_p(nkv).reshape(B, L, 4 * LANES),
                    sl_p(win).reshape(B, L, 2 * LANES), gt, ov_p, min(SEL_TOPN, n_sel_p))
    o = o.reshape(B, NSA_KV_HEADS, NSA_GROUP, L, NSA_KV_HEADS, HEAD_DIM)
    o = jnp.einsum("bkglkd->blkgd", o)
    o_nsa_p = o.reshape(Np, NSA_WIDTH)

    Lk = past + Lq
    n_cmp_s = (Lk - CMP_BLOCK) // CMP_STRIDE + 1
    n_sel_s = -(-Lk // SEL_BLOCK)
    ov_s = _overlap(ck_s.shape[1], _pad_to(n_sel_s, LANES), n_cmp_s, n_sel_s)
    qg = sl_s(nq).reshape(Bd, Lq, NSA_KV_HEADS, NSA_GROUP, HEAD_DIM).transpose(0, 3, 2, 1, 4)
    qm = _head_masked(qg, NSA_KV_HEADS).transpose(0, 2, 1, 3, 4).reshape(Bd, NSA_HEADS * Lq, LANES)
    gt = sl_s(gates)[:, :N_GATE_COLS].reshape(Bd, Lq, NSA_KV_HEADS, NSA_GROUP, 3)
    gt = gt.transpose(0, 2, 3, 1, 4).reshape(Bd, NSA_HEADS * Lq, 3)
    nkv_s = sl_s(nkv).reshape(Bd, Lq, 4 * LANES)
    wrows = win_all_s.reshape(Bd, -1, 2 * LANES)
    wpad = _pad_to(wrows.shape[1], PAGE_SIZE) - wrows.shape[1]
    wrows = jnp.pad(wrows, ((0, 0), (0, wpad), (0, 0)))
    key_blk = jnp.arange(past + PAGE_SIZE) // SEL_BLOCK
    ex = (jnp.arange(ov_s.shape[1])[:, None] == key_blk[None, :]).astype(BF16)
    o = _nsa_decode(page_table, qm, ck_s, cv_s,
                    pad_rows(nkv_s[:, :, 2 * LANES:3 * LANES]), pad_rows(nkv_s[:, :, 3 * LANES:]),
                    nsa_pages, wrows[:, :, :LANES], wrows[:, :, LANES:], gt, ov_s, ex,
                    l * n_pool, Lq, min(SEL_TOPN, n_sel_s))
    o = o.reshape(Bd, NSA_KV_HEADS, NSA_GROUP, Lq, NSA_KV_HEADS, HEAD_DIM)
    o = jnp.einsum("bkgqkd->bqkgd", o)
    o_nsa_s = o.reshape(Ns, NSA_WIDTH)

    tail = jnp.zeros((NT - Np - Ns, SB_WIDTH), F32)
    o_sb = jnp.concatenate([o_sb_p, o_sb_s, tail], axis=0)
    o_nsa = jnp.concatenate([o_nsa_p, o_nsa_s, tail], axis=0)
    x1 = _merge(xcat, o_sb, o_nsa, gab, W["w_br_a"][l], W["w_br_b"][l], W["w_out"][l],
                W["ln1_g"][l], W["ln1_b"][l])
    tt = 512 if NT % 512 == 0 else 256
    a_s, b, thr = _peer_route(x1, W["peer_wq"][l], W["peer_keys"][l], tt)
    x2 = _peer_dense(x1, a_s, b, thr, W["peer_u"][l], W["peer_vt"][l],
                     W["ln2_g"][l], W["ln2_b"][l], tt)

    keep_s = state_win.shape[2]
    keep_p = min(WINDOW, L)
    return x2, (new_sb_p, new_sb_s, new_nsa_p, new_nsa_s, win_p[:, L - keep_p:],
                win_all_s[:, win_all_s.shape[1] - keep_s:])


def _prep_weights(w_in, w_cmp1, w_cmp2, cmp_pe, w_br_a, w_br_b, w_out, ln1_g, ln1_b,
                  peer_wq, peer_keys, peer_u, peer_v, ln2_g, ln2_b):
    depth = w_in.shape[0]
    g0 = 2816
    w_in_p = jnp.concatenate(
        [w_in[:, :, :g0], w_in[:, :, g0 + N_GATE_COLS:], w_in[:, :, g0:g0 + N_GATE_COLS],
         jnp.zeros((depth, D_MODEL, LANES - N_GATE_COLS), w_in.dtype)], axis=2).astype(BF16)
    eye = jnp.eye(NSA_KV_HEADS, dtype=F32)
    cmp = []
    for l in range(depth):
        w1 = w_cmp1[l]
        half = lambda s: jnp.einsum("tjdh,gk->tjgdkh", w1[:, s], eye).reshape(
            2, CMP_STRIDE * LANES, 2 * LANES).astype(BF16)
        pe = cmp_pe[l]
        pe_half = lambda s: jnp.broadcast_to(
            pe[:, s, None, :], (2, CMP_STRIDE, NSA_KV_HEADS, HEAD_DIM)).reshape(2, 1, CMP_STRIDE * LANES)
        w2big = jnp.einsum("thd,gk->tghkd", w_cmp2[l], eye).reshape(2, 2 * LANES, LANES).astype(BF16)
        lo, hi = slice(0, CMP_STRIDE), slice(CMP_STRIDE, CMP_BLOCK)
        cmp.append((half(lo), half(hi), pe_half(lo), pe_half(hi), w2big))
    return {
        "w_in": w_in_p, "cmp": cmp,
        "w_br_a": w_br_a.astype(BF16), "w_br_b": w_br_b.astype(BF16), "w_out": w_out.astype(BF16),
        "ln1_g": ln1_g[:, None, :], "ln1_b": ln1_b[:, None, :],
        "ln2_g": ln2_g[:, None, :], "ln2_b": ln2_b[:, None, :],
        "peer_wq": peer_wq.astype(BF16), "peer_keys": peer_keys.astype(BF16),
        "peer_u": _expert_table_bf16(peer_u, False), "peer_vt": _expert_table_bf16(peer_v, True),
    }


def kernel(x_prompt, x_sample, cache_sb, cache_nsa, state_win, page_table, w_in, w_cmp1, w_cmp2,
           cmp_pe, w_br_a, w_br_b, w_out, ln1_g, ln1_b, peer_wq, peer_keys, peer_u, peer_v,
           ln2_g, ln2_b):
    B, L, D = x_prompt.shape
    Bd, Lq, _ = x_sample.shape
    n_pages = page_table.shape[1]
    past = n_pages * PAGE_SIZE
    Np, Ns = B * L, Bd * Lq
    NT = _pad_to(Np + Ns, 512)
    assert L % 256 == 0 and state_win.shape[2] == WINDOW and past >= WINDOW and Lq == SUBLANES

    W = _prep_weights(w_in, w_cmp1, w_cmp2, cmp_pe, w_br_a, w_br_b, w_out, ln1_g, ln1_b,
                      peer_wq, peer_keys, peer_u, peer_v, ln2_g, ln2_b)
    pos = jnp.concatenate([jnp.tile(jnp.arange(L), B), jnp.tile(past + jnp.arange(Lq), Bd),
                           jnp.zeros((NT - Np - Ns,), jnp.int32)])
    tables = _rope_tables(pos)
    xcat = jnp.concatenate([x_prompt.reshape(Np, D), x_sample.reshape(Ns, D),
                            jnp.zeros((NT - Np - Ns, D), x_prompt.dtype)], axis=0)
    geom = (B, L, Bd, Lq, n_pages)
    outs = []
    for l in range(w_in.shape[0]):
        xcat, leaves = _layer(l, xcat, geom, cache_sb, cache_nsa, state_win, page_table, tables, W)
        outs.append(leaves)
    stack = lambda k: jnp.stack([o[k] for o in outs])
    return (xcat[:Np].reshape(B, L, D), xcat[Np:Np + Ns].reshape(Bd, Lq, D),
            stack(0), stack(1), stack(2), stack(3), stack(4), stack(5))
```
